```python
import jax, jax.numpy as jnp
from jax import lax
import numpy as np

D_MODEL = 4096
BATCH = 32
SEQ = 256
DEPTH = 2
DEC_BATCH = 8
DEC_SEQ = 1024
PAST_LEN = 256

GRID_W = 64
HEAD_DIM = 128
GROUP_WIDTH = D_MODEL // 4
A_HEADS = GROUP_WIDTH // HEAD_DIM
A_KV_HEADS = 2
D_HEADS = GROUP_WIDTH // HEAD_DIM
D_KV_HEADS = 2
B_GROUPS = 8
B_CH = GROUP_WIDTH // B_GROUPS
C_GROUPS = 8
C_CH = GROUP_WIDTH // C_GROUPS
CHUNK = 128
Q_BLOCK = 128
WINDOW = 128
D_FF = 5 * D_MODEL // 2
CONV_W = 3
ROPE_THETA = 10000.0
EPS = 1e-6
IN_SIZES = (A_HEADS * HEAD_DIM, A_KV_HEADS * HEAD_DIM, A_KV_HEADS * HEAD_DIM,
            GROUP_WIDTH, GROUP_WIDTH, GROUP_WIDTH,
            D_HEADS * HEAD_DIM, D_KV_HEADS * HEAD_DIM, D_KV_HEADS * HEAD_DIM)
IN_COLS = sum(IN_SIZES)

kernel_name = 'hymba_style_diffusion_step'


def rmsnorm(x, g):
    xf = x.astype(jnp.float32)
    y = xf * lax.rsqrt(jnp.mean(xf * xf, axis=-1, keepdims=True) + EPS)
    return (y * g.astype(jnp.float32)).astype(x.dtype)


def rope_1d(x, pos):
    half = x.shape[-1] // 2
    inv = ROPE_THETA ** (-jnp.arange(half, dtype=jnp.float32) / half)
    ang = pos.astype(jnp.float32)[:, None] * inv[None, :]
    bshape = (1, pos.shape[0]) + (1,) * (x.ndim - 3) + (half,)
    cos = jnp.cos(ang).reshape(bshape)
    sin = jnp.sin(ang).reshape(bshape)
    xf = x.astype(jnp.float32)
    x1, x2 = xf[..., :half], xf[..., half:]
    return jnp.concatenate([x1 * cos - x2 * sin, x1 * sin + x2 * cos], axis=-1).astype(x.dtype)


def rope_2d(x):
    s = x.shape[1]
    rows = s // GRID_W
    row = jnp.repeat(jnp.arange(rows), GRID_W)
    col = jnp.tile(jnp.arange(GRID_W), rows)
    h = x.shape[-1] // 2
    return jnp.concatenate([rope_1d(x[..., :h], row), rope_1d(x[..., h:], col)], axis=-1)


def dense_attention(q, k, v, sink):
    b, s, n_kv, g, hd = q.shape
    nb = s // Q_BLOCK
    scale = hd ** -0.5
    qb = jnp.moveaxis(q.reshape(b, nb, Q_BLOCK, n_kv, g, hd), 1, 0)

    def block(qblk):
        sc = jnp.einsum('bqkgd,btkd->bkgqt', qblk, k).astype(jnp.float32) * scale
        if sink is None:
            p = jax.nn.softmax(sc, axis=-1)
        else:
            sk = jnp.broadcast_to(sink.astype(jnp.float32).reshape(1, n_kv, g, 1, 1), sc.shape[:-1] + (1,))
            p = jax.nn.softmax(jnp.concatenate([sc, sk], axis=-1), axis=-1)[..., :-1]
        return jnp.einsum('bkgqt,btkd->bqkgd', p.astype(v.dtype), v)

    o = lax.map(block, qb)
    return jnp.moveaxis(o, 0, 1).reshape(b, s, n_kv * g * hd)


def window_attention(q, k, v, k_ctx, v_ctx, sink):
    b, s, n_kv, g, hd = q.shape
    nb = s // Q_BLOCK
    scale = hd ** -0.5

    def windows(t):
        tb = jnp.pad(t, ((0, 0), (Q_BLOCK, Q_BLOCK), (0, 0), (0, 0))).reshape(b, nb + 2, Q_BLOCK, n_kv, hd)
        return jnp.concatenate([tb[:, :-2], tb[:, 1:-1], tb[:, 2:]], axis=2)

    kw, vw = windows(k), windows(v)
    qb = q.reshape(b, nb, Q_BLOCK, n_kv, g, hd)
    s_win = jnp.einsum('bnqkgd,bnwkd->bnkgqw', qb, kw).astype(jnp.float32) * scale
    a = jnp.arange(Q_BLOCK)[:, None]
    j = jnp.arange(3 * Q_BLOCK)[None, :]
    blk = jnp.arange(nb)[:, None, None]
    q_pos = blk * Q_BLOCK + a
    k_pos = blk * Q_BLOCK - Q_BLOCK + j
    valid = (jnp.abs(q_pos - k_pos) <= WINDOW) & (k_pos >= 0) & (k_pos < s)
    s_win = jnp.where(valid[None, :, None, None], s_win, -jnp.inf)
    s_ctx = jnp.einsum('bnqkgd,btkd->bnkgqt', qb, k_ctx).astype(jnp.float32) * scale
    sk = jnp.broadcast_to(sink.astype(jnp.float32).reshape(1, 1, n_kv, g, 1, 1), s_win.shape[:-1] + (1,))
    p = jax.nn.softmax(jnp.concatenate([s_win, s_ctx, sk], axis=-1), axis=-1)
    w3 = 3 * Q_BLOCK
    t = k_ctx.shape[1]
    p_win = p[..., :w3].astype(v.dtype)
    p_ctx = p[..., w3:w3 + t].astype(v.dtype)
    o = (jnp.einsum('bnkgqw,bnwkd->bnqkgd', p_win, vw)
         + jnp.einsum('bnkgqt,btkd->bnqkgd', p_ctx, v_ctx))
    return o.reshape(b, s, n_kv * g * hd)


def fourier_mix(f):
    b, s, _ = f.shape
    fg = f.reshape(b, s, B_GROUPS, B_CH).astype(jnp.float32)
    out = jnp.fft.fft2(fg, axes=(1, 3), norm='ortho').real
    return out.reshape(b, s, B_GROUPS * B_CH).astype(f.dtype)


def spatial_gating(u, v, gain, w_s, b_s):
    b, s, _ = u.shape
    n = s // CHUNK
    u = jax.nn.gelu(u).reshape(b, n, CHUNK, C_GROUPS, C_CH)
    v = rmsnorm(jax.nn.gelu(v).reshape(b, s, C_GROUPS, C_CH), gain).reshape(b, n, CHUNK, C_GROUPS, C_CH)
    sv = jnp.einsum('gpq,bnqgc->bnpgc', w_s, v) + b_s[None, None, :, :, None]
    return (u * sv).reshape(b, s, C_GROUPS * C_CH)


def conv_ffn(h, w_up, conv_w, conv_b, w_down):
    z = h @ w_up
    zp = jnp.pad(z, ((0, 0), (1, 1), (0, 0)))
    z = zp[:, :-2] * conv_w[0] + zp[:, 1:-1] * conv_w[1] + zp[:, 2:] * conv_w[2] + conv_b
    gate, up = jnp.split(z, 2, axis=-1)
    return (jax.nn.silu(gate) * up) @ w_down


def trunk_layer(x, cond, ctx_kv, w_ada, b_ada, g_mix, w_in, qn_a, kn_a, qn_d, kn_d, sink_d,
                gn_c, w_s, b_s, w_o, g_ffn, w_up, conv_w, conv_b, w_down):
    b, s, _ = x.shape
    mod = jax.nn.silu(cond) @ w_ada + b_ada
    sh1, sc1, gt1, sh2, sc2, gt2 = [m[:, None, :] for m in jnp.split(mod, 6, axis=-1)]
    h = rmsnorm(x, g_mix) * (1 + sc1) + sh1
    p = h @ w_in
    parts = []
    off = 0
    for n in IN_SIZES:
        parts.append(p[..., off:off + n])
        off += n
    qa, ka, va, fb, uc, vc, qd, kd, vd = parts
    qa = rmsnorm(qa.reshape(b, s, A_KV_HEADS, A_HEADS // A_KV_HEADS, HEAD_DIM), qn_a)
    ka = rmsnorm(ka.reshape(b, s, A_KV_HEADS, HEAD_DIM), kn_a)
    va = va.reshape(b, s, A_KV_HEADS, HEAD_DIM)
    qd = rmsnorm(qd.reshape(b, s, D_KV_HEADS, D_HEADS // D_KV_HEADS, HEAD_DIM), qn_d)
    kd = rmsnorm(kd.reshape(b, s, D_KV_HEADS, HEAD_DIM), kn_d)
    vd = vd.reshape(b, s, D_KV_HEADS, HEAD_DIM)
    if ctx_kv is None:
        oa = dense_attention(qa, ka, va, None)
        od = dense_attention(qd, kd, vd, sink_d)
    else:
        ka_c, va_c, kd_c, vd_c = ctx_kv
        oa = dense_attention(rope_2d(qa), jnp.concatenate([ka_c, rope_2d(ka)], axis=1),
                             jnp.concatenate([va_c, va], axis=1), None)
        od = window_attention(rope_2d(qd), rope_2d(kd), vd, kd_c, vd_c, sink_d)
    ob = fourier_mix(fb)
    oc = spatial_gating(uc, vc, gn_c, w_s, b_s)
    x = x + gt1 * (jnp.concatenate([oa, ob, oc, od], axis=-1) @ w_o)
    h2 = rmsnorm(x, g_ffn) * (1 + sc2) + sh2
    x = x + gt2 * conv_ffn(h2, w_up, conv_w, conv_b, w_down)
    return x, (ka, va, kd, vd)


def setup_inputs(seed: int = 0) -> dict:
    key = jax.random.key(seed)
    ks = iter(jax.random.split(key, 32))

    def nrm(shape, scale):
        return jax.random.normal(next(ks), shape, jnp.float32) * scale

    ca = (DEC_BATCH, DEPTH, PAST_LEN, A_KV_HEADS, HEAD_DIM)
    cd = (DEC_BATCH, DEPTH, PAST_LEN, D_KV_HEADS, HEAD_DIM)
    return {
        'x_prompt': nrm((BATCH, SEQ, D_MODEL), 1.0),
        'x_sample': nrm((DEC_BATCH, DEC_SEQ, D_MODEL), 1.0),
        'c': nrm((DEC_BATCH, D_MODEL), 1.0),
        'cache_ga_k': nrm(ca, 1.0),
        'cache_ga_v': nrm(ca, 1.0),
        'cache_sw_k': nrm(cd, 1.0),
        'cache_sw_v': nrm(cd, 1.0),
        'c_ctx': nrm((D_MODEL,), 1.0),
        'w_ada': nrm((DEPTH, D_MODEL, 6 * D_MODEL), 0.5 * D_MODEL ** -0.5),
        'b_ada': nrm((DEPTH, 6 * D_MODEL), 0.02),
        'g_mix': 1.0 + nrm((DEPTH, D_MODEL), 0.02),
        'w_in': nrm((DEPTH, D_MODEL, IN_COLS), D_MODEL ** -0.5),
        'qn_a': 1.0 + nrm((DEPTH, HEAD_DIM), 0.02),
        'kn_a': 1.0 + nrm((DEPTH, HEAD_DIM), 0.02),
        'qn_d': 1.0 + nrm((DEPTH, HEAD_DIM), 0.02),
        'kn_d': 1.0 + nrm((DEPTH, HEAD_DIM), 0.02),
        'sink_d': nrm((DEPTH, D_HEADS), 0.5),
        'gn_c': 1.0 + nrm((DEPTH, C_GROUPS, C_CH), 0.02),
        'w_s': nrm((DEPTH, C_GROUPS, CHUNK, CHUNK), CHUNK ** -0.5),
        'b_s': nrm((DEPTH, CHUNK, C_GROUPS), 0.02),
        'w_o': nrm((DEPTH, D_MODEL, D_MODEL), D_MODEL ** -0.5),
        'g_ffn': 1.0 + nrm((DEPTH, D_MODEL), 0.02),
        'w_up': nrm((DEPTH, D_MODEL, 2 * D_FF), D_MODEL ** -0.5),
        'conv_w': nrm((DEPTH, CONV_W, 2 * D_FF), CONV_W ** -0.5),
        'conv_b': nrm((DEPTH, 2 * D_FF), 0.02),
        'w_down': nrm((DEPTH, D_FF, D_MODEL), D_FF ** -0.5),
    }


def reference(x_prompt, x_sample, c, cache_ga_k, cache_ga_v, cache_sw_k, cache_sw_v, c_ctx,
              w_ada, b_ada, g_mix, w_in, qn_a, kn_a, qn_d, kn_d, sink_d, gn_c, w_s, b_s,
              w_o, g_ffn, w_up, conv_w, conv_b, w_down):
    h = x_prompt
    cond_ctx = c_ctx[None, :]
    ka_l, va_l, kd_l, vd_l = [], [], [], []
    for l in range(DEPTH):
        h, (ka, va, kd, vd) = trunk_layer(
            h, cond_ctx, None, w_ada[l], b_ada[l], g_mix[l], w_in[l], qn_a[l], kn_a[l], qn_d[l], kn_d[l],
            sink_d[l], gn_c[l], w_s[l], b_s[l], w_o[l], g_ffn[l], w_up[l], conv_w[l], conv_b[l], w_down[l])
        ka_l.append(ka)
        va_l.append(va)
        kd_l.append(kd)
        vd_l.append(vd)
    y_prompt = h
    new_ga_k = jnp.stack(ka_l, axis=1)
    new_ga_v = jnp.stack(va_l, axis=1)
    new_sw_k = jnp.stack(kd_l, axis=1)
    new_sw_v = jnp.stack(vd_l, axis=1)

    z = x_sample
    for l in range(DEPTH):
        ctx = (cache_ga_k[:, l], cache_ga_v[:, l], cache_sw_k[:, l], cache_sw_v[:, l])
        z, _ = trunk_layer(
            z, c, ctx, w_ada[l], b_ada[l], g_mix[l], w_in[l], qn_a[l], kn_a[l], qn_d[l], kn_d[l],
            sink_d[l], gn_c[l], w_s[l], b_s[l], w_o[l], g_ffn[l], w_up[l], conv_w[l], conv_b[l], w_down[l])
    y_sample = z
    return (y_prompt, y_sample, new_ga_k, new_ga_v, new_sw_k, new_sw_v)
```

```python
import functools
import math

import jax
import jax.numpy as jnp
from jax import lax
from jax.experimental import pallas as pl
from jax.experimental.pallas import tpu as pltpu

F32 = jnp.float32
BF16 = jnp.bfloat16

HEAD_DIM = 128
GRID_W = 64
CHUNK = 128
WINDOW = 128
ROPE_THETA = 10000.0
EPS = 1e-6
N_MIXERS = 4
FOURIER_GROUPS = 8
N_MOD = 6
MOD_ROWS = 16

V7X_VMEM_BYTES = 64 * 1024 * 1024
VMEM_LIMIT = V7X_VMEM_BYTES - 8 * 1024 * 1024

TILE_M = 1024
TILE_N = 1024
TILE_N_OUT = 512
TILE_FF = 512
TILE_K_DOWN = 2048
NORM_ROWS = 512
ATTN_Q_BLOCK = 256


def _params(*sem):
    return pltpu.CompilerParams(dimension_semantics=sem, vmem_limit_bytes=VMEM_LIMIT)


def _dot(a, b):
    return jnp.dot(a, b, preferred_element_type=F32)


def _dot_nt(a, b):
    return lax.dot_general(a, b, (((1,), (1,)), ((), ())), preferred_element_type=F32)


def _rms(x, gain):
    ms = jnp.mean(x * x, axis=-1, keepdims=True)
    return (x * lax.rsqrt(ms + EPS)) * gain


def _silu(x):
    return x * (1.0 / (1.0 + jnp.exp(-x)))


def _gelu_tanh(x):
    c = math.sqrt(2.0 / math.pi)
    return x * (0.5 * (1.0 + jnp.tanh(c * (x + 0.044715 * (x * x * x)))))


def _ada_kernel(c_ref, w_ref, b_ref, o_ref):
    a = _silu(c_ref[...]).astype(BF16)
    o_ref[...] = _dot(a, w_ref[...].astype(BF16)) + b_ref[...]


def _ada_mod(cond, w_ada, b_ada):
    n_layers, d, n = w_ada.shape
    tn = TILE_N
    return pl.pallas_call(
        _ada_kernel,
        grid=(n_layers, n // tn),
        in_specs=[
            pl.BlockSpec((MOD_ROWS, d), lambda l, j: (0, 0)),
            pl.BlockSpec((None, d, tn), lambda l, j: (l, 0, j)),
            pl.BlockSpec((None, 1, tn), lambda l, j: (l, 0, j)),
        ],
        out_specs=pl.BlockSpec((None, MOD_ROWS, tn), lambda l, j: (l, 0, j)),
        out_shape=jax.ShapeDtypeStruct((n_layers, MOD_ROWS, n), F32),
        compiler_params=_params("parallel", "parallel"),
        name="ada_mod",
    )(cond, w_ada, b_ada.reshape(n_layers, 1, n))


def _mod_spec(layer, chunk, row_of, width, col_of):
    return pl.BlockSpec(
        (None, None, None, 1, width),
        lambda *g: (layer, row_of(*g), chunk, 0, col_of(*g)))


def _norm_mod_kernel(x_ref, g_ref, sc_ref, sh_ref, o_ref):
    y = _rms(x_ref[...], g_ref[...])
    o_ref[...] = (y * (1.0 + sc_ref[...]) + sh_ref[...]).astype(o_ref.dtype)


def _norm_mod(x, gain, mod, layer, sc_chunk, sh_chunk, row_of_tile):
    m, d = x.shape
    bm = NORM_ROWS
    row_of = lambda i: row_of_tile(i, bm)
    zero = lambda i: 0
    return pl.pallas_call(
        _norm_mod_kernel,
        grid=(m // bm,),
        in_specs=[
            pl.BlockSpec((bm, d), lambda i: (i, 0)),
            pl.BlockSpec((None, 1, d), lambda i: (layer, 0, 0)),
            _mod_spec(layer, sc_chunk, row_of, d, zero),
            _mod_spec(layer, sh_chunk, row_of, d, zero),
        ],
        out_specs=pl.BlockSpec((bm, d), lambda i: (i, 0)),
        out_shape=jax.ShapeDtypeStruct((m, d), BF16),
        compiler_params=_params("parallel"),
        name="norm_mod",
    )(x, gain.reshape(gain.shape[0], 1, d), mod, mod)


def _proj_kernel(x_ref, w_ref, o_ref):
    o_ref[...] = _dot(x_ref[...], w_ref[...]).astype(o_ref.dtype)


def _in_proj(h, w, layer):
    m, k = h.shape
    n = w.shape[-1]
    return pl.pallas_call(
        _proj_kernel,
        grid=(m // TILE_M, n // TILE_N),
        in_specs=[
            pl.BlockSpec((TILE_M, k), lambda i, j: (i, 0)),
            pl.BlockSpec((None, k, TILE_N), lambda i, j: (layer, 0, j)),
        ],
        out_specs=pl.BlockSpec((TILE_M, TILE_N), lambda i, j: (i, j)),
        out_shape=jax.ShapeDtypeStruct((m, n), F32),
        compiler_params=_params("parallel", "parallel"),
        name="in_proj",
    )(h, w)


def _out_proj_kernel(a_ref, b_ref, c_ref, d_ref, w_ref, x_ref, gt_ref, o_ref):
    kw = a_ref.shape[1]
    acc = _dot(a_ref[...], w_ref[0 * kw:1 * kw, :])
    acc += _dot(b_ref[...], w_ref[1 * kw:2 * kw, :])
    acc += _dot(c_ref[...], w_ref[2 * kw:3 * kw, :])
    acc += _dot(d_ref[...], w_ref[3 * kw:4 * kw, :])
    o_ref[...] = x_ref[...] + gt_ref[...] * acc


def _out_proj(mixed, w, x, mod, layer, gt_chunk, row_of_tile):
    m, d = x.shape
    kw = mixed[0].shape[1]
    tn = TILE_N_OUT
    row_of = lambda i, j: row_of_tile(i, TILE_M)
    mix_spec = pl.BlockSpec((TILE_M, kw), lambda i, j: (i, 0))
    return pl.pallas_call(
        _out_proj_kernel,
        grid=(m // TILE_M, d // tn),
        in_specs=[mix_spec] * N_MIXERS + [
            pl.BlockSpec((None, N_MIXERS * kw, tn), lambda i, j: (layer, 0, j)),
            pl.BlockSpec((TILE_M, tn), lambda i, j: (i, j)),
            _mod_spec(layer, gt_chunk, row_of, tn, lambda i, j: j),
        ],
        out_specs=pl.BlockSpec((TILE_M, tn), lambda i, j: (i, j)),
        out_shape=jax.ShapeDtypeStruct((m, d), F32),
        compiler_params=_params("parallel", "parallel"),
        name="out_proj",
    )(*mixed, w, x, mod)


def _ffn_up_kernel(h_ref, wg_ref, wu_ref, cwg_ref, cwu_ref, cbg_ref, cbu_ref, o_ref, *, seq_len):
    h = h_ref[...]
    rows = h.shape[0]
    pos = lax.rem(lax.broadcasted_iota(jnp.int32, (rows, 1), 0), seq_len)
    first = pos == 0
    last = pos == seq_len - 1

    def conv(z, cw_ref, cb_ref):
        prev = jnp.where(first, 0.0, pltpu.roll(z, 1, 0))
        nxt = jnp.where(last, 0.0, pltpu.roll(z, rows - 1, 0))
        return prev * cw_ref[0:1, :] + z * cw_ref[1:2, :] + nxt * cw_ref[2:3, :] + cb_ref[...]

    gate = conv(_dot(h, wg_ref[...]), cwg_ref, cbg_ref)
    up = conv(_dot(h, wu_ref[...]), cwu_ref, cbu_ref)
    o_ref[...] = (_silu(gate) * up).astype(o_ref.dtype)


def _ffn_up(h, w_up, conv_w, conv_b, layer, seq_len):
    m, k = h.shape
    d_ff = w_up.shape[-1] // 2
    tn = TILE_FF
    nj = d_ff // tn
    assert TILE_M % seq_len == 0
    w_spec = lambda off: pl.BlockSpec((None, k, tn), lambda i, j: (layer, 0, off + j))
    cw_spec = lambda off: pl.BlockSpec((None, conv_w.shape[1], tn), lambda i, j: (layer, 0, off + j))
    cb_spec = lambda off: pl.BlockSpec((None, 1, tn), lambda i, j: (layer, 0, off + j))
    conv_b3 = conv_b.reshape(conv_b.shape[0], 1, 2 * d_ff)
    return pl.pallas_call(
        functools.partial(_ffn_up_kernel, seq_len=seq_len),
        grid=(m // TILE_M, nj),
        in_specs=[
            pl.BlockSpec((TILE_M, k), lambda i, j: (i, 0)),
            w_spec(0), w_spec(nj), cw_spec(0), cw_spec(nj), cb_spec(0), cb_spec(nj),
        ],
        out_specs=pl.BlockSpec((TILE_M, tn), lambda i, j: (i, j)),
        out_shape=jax.ShapeDtypeStruct((m, d_ff), BF16),
        compiler_params=_params("parallel", "parallel"),
        name="ffn_up",
    )(h, w_up, w_up, conv_w, conv_w, conv_b3, conv_b3)


def _ffn_down_kernel(a_ref, w_ref, x_ref, gt_ref, o_ref, acc_ref):
    k = pl.program_id(2)

    @pl.when(k == 0)
    def _():
        acc_ref[...] = jnp.zeros_like(acc_ref)

    acc_ref[...] += _dot(a_ref[...], w_ref[...])

    @pl.when(k == pl.num_programs(2) - 1)
    def _():
        o_ref[...] = x_ref[...] + gt_ref[...] * acc_ref[...]


def _ffn_down(act, w, x, mod, layer, gt_chunk, row_of_tile):
    m, d = x.shape
    d_ff = act.shape[1]
    tk = TILE_K_DOWN
    row_of = lambda i, j, k: row_of_tile(i, TILE_M)
    return pl.pallas_call(
        _ffn_down_kernel,
        grid=(m // TILE_M, d // TILE_N, d_ff // tk),
        in_specs=[
            pl.BlockSpec((TILE_M, tk), lambda i, j, k: (i, k)),
            pl.BlockSpec((None, tk, TILE_N), lambda i, j, k: (layer, k, j)),
            pl.BlockSpec((TILE_M, TILE_N), lambda i, j, k: (i, j)),
            _mod_spec(layer, gt_chunk, row_of, TILE_N, lambda i, j, k: j),
        ],
        out_specs=pl.BlockSpec((TILE_M, TILE_N), lambda i, j, k: (i, j)),
        out_shape=jax.ShapeDtypeStruct((m, d), F32),
        scratch_shapes=[pltpu.VMEM((TILE_M, TILE_N), F32)],
        compiler_params=_params("parallel", "parallel", "arbitrary"),
        name="ffn_down",
    )(act, w, x, mod)


def _softmax_pv(s, v, sink):
    m = jnp.max(s, axis=-1, keepdims=True)
    if sink is not None:
        m = jnp.maximum(m, sink)
    e = jnp.exp(s - m)
    den = jnp.sum(e, axis=-1, keepdims=True)
    if sink is not None:
        den = den + jnp.exp(sink - m)
    return _dot(e.astype(BF16), v) / den


def _sink_column(sink_ref, kv, groups, rows):
    return jnp.concatenate(
        [jnp.full((rows, 1), sink_ref[kv * groups + g], F32) for g in range(groups)], axis=0)


def _attn_ctx_kernel(*refs, groups, has_sink):
    if has_sink:
        q_ref, k_ref, v_ref, qn_ref, kn_ref, sink_ref, o_ref, ko_ref, vo_ref = refs
    else:
        q_ref, k_ref, v_ref, qn_ref, kn_ref, o_ref, ko_ref, vo_ref = refs
    s_len = q_ref.shape[0]
    kv = pl.program_id(1)
    k = _rms(k_ref[...], kn_ref[...])
    v = v_ref[...]
    ko_ref[...] = k
    vo_ref[...] = v
    qn = qn_ref[...]
    q = jnp.concatenate(
        [_rms(q_ref[:, g * HEAD_DIM:(g + 1) * HEAD_DIM], qn) for g in range(groups)], axis=0)
    s = _dot_nt(q.astype(BF16), k.astype(BF16)) * (HEAD_DIM ** -0.5)
    sink = _sink_column(sink_ref, kv, groups, s_len) if has_sink else None
    o = _softmax_pv(s, v.astype(BF16), sink)
    for g in range(groups):
        o_ref[:, g * HEAD_DIM:(g + 1) * HEAD_DIM] = o[g * s_len:(g + 1) * s_len].astype(o_ref.dtype)


def _attn_ctx(p, col0, n_kv, groups, s_len, qn, kn, sink):
    m = p.shape[0]
    qw = groups * HEAD_DIM
    k0 = (col0 + n_kv * qw) // HEAD_DIM
    v0 = k0 + n_kv
    has_sink = sink is not None
    vec = pl.BlockSpec((1, HEAD_DIM), lambda b, h: (0, 0))
    in_specs = [
        pl.BlockSpec((s_len, qw), lambda b, h: (b, col0 // qw + h)),
        pl.BlockSpec((s_len, HEAD_DIM), lambda b, h: (b, k0 + h)),
        pl.BlockSpec((s_len, HEAD_DIM), lambda b, h: (b, v0 + h)),
        vec, vec,
    ]
    args = [p, p, p, qn.reshape(1, HEAD_DIM), kn.reshape(1, HEAD_DIM)]
    if has_sink:
        in_specs.append(pl.BlockSpec(memory_space=pltpu.SMEM))
        args.append(sink)
    kv_spec = pl.BlockSpec((s_len, HEAD_DIM), lambda b, h: (b, h))
    kv_shape = jax.ShapeDtypeStruct((m, n_kv * HEAD_DIM), F32)
    return pl.pallas_call(
        functools.partial(_attn_ctx_kernel, groups=groups, has_sink=has_sink),
        grid=(m // s_len, n_kv),
        in_specs=in_specs,
        out_specs=[pl.BlockSpec((s_len, qw), lambda b, h: (b, h)), kv_spec, kv_spec],
        out_shape=[jax.ShapeDtypeStruct((m, n_kv * qw), BF16), kv_shape, kv_shape],
        compiler_params=_params("parallel", "parallel"),
        name="attn_ctx",
    )(*args)


def _rope(x, cos, sin_signed):
    lane = lax.broadcasted_iota(jnp.int32, x.shape, 1)
    partner = jnp.where((lane & (HEAD_DIM // 4)) == 0,
                        pltpu.roll(x, HEAD_DIM - HEAD_DIM // 4, 1),
                        pltpu.roll(x, HEAD_DIM // 4, 1))
    return x * cos + partner * sin_signed


def _attn_lat_kernel(*refs, groups, has_sink, window, n_past):
    if has_sink:
        (q_ref, k_ref, v_ref, kc_ref, vc_ref, cq_ref, sq_ref, ck_ref, sk_ref, qn_ref, kn_ref,
         sink_ref, o_ref, kbuf, vbuf) = refs
    else:
        (q_ref, k_ref, v_ref, kc_ref, vc_ref, cq_ref, sq_ref, ck_ref, sk_ref, qn_ref, kn_ref,
         o_ref, kbuf, vbuf) = refs
    qb = q_ref.shape[0]
    s_len = k_ref.shape[0]
    kv = pl.program_id(1)
    blk = pl.program_id(2)

    @pl.when(blk == 0)
    def _():
        kbuf[0:n_past, :] = kc_ref[...].astype(BF16)
        vbuf[0:n_past, :] = vc_ref[...].astype(BF16)
        k = _rope(_rms(k_ref[...], kn_ref[...]), ck_ref[...], sk_ref[...])
        kbuf[n_past:n_past + s_len, :] = k.astype(BF16)
        vbuf[n_past:n_past + s_len, :] = v_ref[...].astype(BF16)

    qn = qn_ref[...]
    cos = cq_ref[...]
    sin = sq_ref[...]
    q = jnp.concatenate(
        [_rope(_rms(q_ref[:, g * HEAD_DIM:(g + 1) * HEAD_DIM], qn), cos, sin)
         for g in range(groups)], axis=0).astype(BF16)
    if window is None:
        keys = kbuf[...]
        vals = vbuf[...]
        s = _dot_nt(q, keys) * (HEAD_DIM ** -0.5)
    else:
        span = qb + 2 * window
        start = pl.multiple_of(jnp.clip(blk * qb - window, 0, s_len - span), window)
        keys = jnp.concatenate([kbuf[0:n_past, :], kbuf[pl.ds(n_past + start, span), :]], axis=0)
        vals = jnp.concatenate([vbuf[0:n_past, :], vbuf[pl.ds(n_past + start, span), :]], axis=0)
        s = _dot_nt(q, keys) * (HEAD_DIM ** -0.5)
        col = lax.broadcasted_iota(jnp.int32, s.shape, 1)
        q_pos = blk * qb + lax.rem(lax.broadcasted_iota(jnp.int32, s.shape, 0), qb)
        k_pos = start + col - n_past
        valid = (col < n_past) | (jnp.abs(q_pos - k_pos) <= window)
        s = jnp.where(valid, s, -jnp.inf)
    sink = _sink_column(sink_ref, kv, groups, qb) if has_sink else None
    o = _softmax_pv(s, vals, sink)
    for g in range(groups):
        o_ref[:, g * HEAD_DIM:(g + 1) * HEAD_DIM] = o[g * qb:(g + 1) * qb].astype(o_ref.dtype)


def _attn_lat(p, col0, n_kv, groups, s_len, cache_k, cache_v, layer, cos, sin_signed,
              qn, kn, sink, window):
    m = p.shape[0]
    qw = groups * HEAD_DIM
    qb = ATTN_Q_BLOCK
    nq = s_len // qb
    n_past = cache_k.shape[2]
    k0 = (col0 + n_kv * qw) // HEAD_DIM
    v0 = k0 + n_kv
    has_sink = sink is not None
    vec = pl.BlockSpec((1, HEAD_DIM), lambda b, h, t: (0, 0))
    cache_spec = pl.BlockSpec((None, None, n_past, HEAD_DIM), lambda b, h, t: (b, layer, 0, h))
    tab_q = pl.BlockSpec((qb, HEAD_DIM), lambda b, h, t: (t, 0))
    tab_k = pl.BlockSpec((s_len, HEAD_DIM), lambda b, h, t: (0, 0))
    in_specs = [
        pl.BlockSpec((qb, qw), lambda b, h, t: (b * nq + t, col0 // qw + h)),
        pl.BlockSpec((s_len, HEAD_DIM), lambda b, h, t: (b, k0 + h)),
        pl.BlockSpec((s_len, HEAD_DIM), lambda b, h, t: (b, v0 + h)),
        cache_spec, cache_spec, tab_q, tab_q, tab_k, tab_k, vec, vec,
    ]
    args = [p, p, p, cache_k, cache_v, cos, sin_signed, cos, sin_signed,
            qn.reshape(1, HEAD_DIM), kn.reshape(1, HEAD_DIM)]
    if has_sink:
        in_specs.append(pl.BlockSpec(memory_space=pltpu.SMEM))
        args.append(sink)
    return pl.pallas_call(
        functools.partial(_attn_lat_kernel, groups=groups, has_sink=has_sink, window=window,
                          n_past=n_past),
        grid=(m // s_len, n_kv, nq),
        in_specs=in_specs,
        out_specs=pl.BlockSpec((qb, qw), lambda b, h, t: (b * nq + t, h)),
        out_shape=jax.ShapeDtypeStruct((m, n_kv * qw), BF16),
        scratch_shapes=[pltpu.VMEM((n_past + s_len, HEAD_DIM), BF16),
                        pltpu.VMEM((n_past + s_len, HEAD_DIM), BF16)],
        compiler_params=_params("parallel", "parallel", "arbitrary"),
        name="attn_lat",
    )(*args)


def _rope_tables(s_len):
    quarter = HEAD_DIM // 4
    inv = ROPE_THETA ** (-jnp.arange(quarter, dtype=F32) / quarter)
    t = jnp.arange(s_len)
    ang_row = (t // GRID_W).astype(F32)[:, None] * inv[None, :]
    ang_col = (t % GRID_W).astype(F32)[:, None] * inv[None, :]
    cos = jnp.concatenate([jnp.cos(ang_row)] * 2 + [jnp.cos(ang_col)] * 2, axis=-1)
    sin = jnp.concatenate([-jnp.sin(ang_row), jnp.sin(ang_row),
                           -jnp.sin(ang_col), jnp.sin(ang_col)], axis=-1)
    return cos, sin


def _fourier_kernel(x_ref, cs_ref, f_ref, o_ref, *, ch, scale):
    x = x_ref[...].astype(BF16)
    cs = cs_ref[...]
    parts = [_dot(x[:, g * ch:(g + 1) * ch], cs) for g in range(x.shape[1] // ch)]
    y = jnp.concatenate(
        [jnp.concatenate([t[:, :ch] for t in parts], axis=1),
         jnp.concatenate([t[:, ch:] for t in parts], axis=1)], axis=0).astype(BF16)
    o_ref[...] = (_dot(f_ref[...], y) * scale).astype(o_ref.dtype)


def _dft_cos_sin(n):
    j = jnp.arange(n, dtype=jnp.int32)
    ang = ((j[:, None] * j[None, :]) % n).astype(F32) * (2.0 * math.pi / n)
    return jnp.cos(ang), jnp.sin(ang)


def _fourier(p, col0, width, ch, s_len):
    m = p.shape[0]
    tw = 512
    cc, sc = _dft_cos_sin(ch)
    cs_mat = jnp.concatenate([cc, sc], axis=1).astype(BF16)
    cp, sp = _dft_cos_sin(s_len)
    f_mat = jnp.concatenate([cp, -sp], axis=1).astype(BF16)
    return pl.pallas_call(
        functools.partial(_fourier_kernel, ch=ch, scale=1.0 / math.sqrt(s_len * ch)),
        grid=(m // s_len, width // tw),
        in_specs=[
            pl.BlockSpec((s_len, tw), lambda b, j: (b, col0 // tw + j)),
            pl.BlockSpec((ch, 2 * ch), lambda b, j: (0, 0)),
            pl.BlockSpec((s_len, 2 * s_len), lambda b, j: (0, 0)),
        ],
        out_specs=pl.BlockSpec((s_len, tw), lambda b, j: (b, j)),
        out_shape=jax.ShapeDtypeStruct((m, width), BF16),
        compiler_params=_params("parallel", "parallel"),
        name="fourier",
    )(p, cs_mat, f_mat)


def _gating_kernel(u_ref, v_ref, gn_ref, w_ref, b_ref, o_ref, *, ch):
    rows = u_ref.shape[0]
    n_chunks = rows // CHUNK
    u = _gelu_tanh(u_ref[...])
    v = _gelu_tanh(v_ref[...])
    for g in range(u.shape[1] // ch):
        lanes = slice(g * ch, (g + 1) * ch)
        vn = _rms(v[:, lanes], gn_ref[:, lanes]).astype(BF16)
        rhs = jnp.concatenate([vn[c * CHUNK:(c + 1) * CHUNK] for c in range(n_chunks)], axis=1)
        sv = _dot(w_ref[g].astype(BF16), rhs)
        bias = b_ref[g]
        for c in range(n_chunks):
            rs = slice(c * CHUNK, (c + 1) * CHUNK)
            o_ref[rs, lanes] = (u[rs, lanes] * (sv[:, c * ch:(c + 1) * ch] + bias)).astype(o_ref.dtype)


def _gating(p, u_col0, v_col0, width, ch, gn, w_s, b_s, layer):
    m = p.shape[0]
    tw = 512
    rows = 512
    gpt = tw // ch
    return pl.pallas_call(
        functools.partial(_gating_kernel, ch=ch),
        grid=(m // rows, width // tw),
        in_specs=[
            pl.BlockSpec((rows, tw), lambda i, j: (i, u_col0 // tw + j)),
            pl.BlockSpec((rows, tw), lambda i, j: (i, v_col0 // tw + j)),
            pl.BlockSpec((None, 1, tw), lambda i, j: (layer, 0, j)),
            pl.BlockSpec((None, gpt, CHUNK, CHUNK), lambda i, j: (layer, j, 0, 0)),
            pl.BlockSpec((None, gpt, CHUNK, ch), lambda i, j: (layer, j, 0, 0)),
        ],
        out_specs=pl.BlockSpec((rows, tw), lambda i, j: (i, j)),
        out_shape=jax.ShapeDtypeStruct((m, width), BF16),
        compiler_params=_params("parallel", "parallel"),
        name="spatial_gating",
    )(p, p, gn, w_s, b_s)


def kernel(x_prompt, x_sample, c, cache_ga_k, cache_ga_v, cache_sw_k, cache_sw_v, c_ctx, w_ada, b_ada, g_mix, w_in, qn_a, kn_a, qn_d, kn_d, sink_d, gn_c, w_s, b_s, w_o, g_ffn, w_up, conv_w, conv_b, w_down):
    n_layers = w_ada.shape[0]
    n_ctx, s_ctx, d = x_prompt.shape
    n_lat, s_lat, _ = x_sample.shape
    gw = d // N_MIXERS
    n_heads = gw // HEAD_DIM
    a_kv = cache_ga_k.shape[3]
    d_kv = cache_sw_k.shape[3]
    b_ch = gw // FOURIER_GROUPS
    c_ch = gn_c.shape[2]
    n_past = cache_ga_k.shape[2]
    assert n_lat + 1 <= MOD_ROWS and s_lat % GRID_W == 0

    a_col = 0
    b_col = a_col + gw + 2 * a_kv * HEAD_DIM
    u_col = b_col + gw
    v_col = u_col + gw
    d_col = v_col + gw

    w_in_b = w_in.astype(BF16)
    w_o_b = w_o.astype(BF16)
    w_up_b = w_up.astype(BF16)
    w_down_b = w_down.astype(BF16)

    cond = jnp.concatenate([c_ctx[None, :], c], axis=0)
    cond = jnp.pad(cond, ((0, MOD_ROWS - cond.shape[0]), (0, 0)))
    mod = _ada_mod(cond, w_ada, b_ada).reshape(n_layers, MOD_ROWS, N_MOD, 1, d)

    cos, sin_signed = _rope_tables(s_lat)
    gn = gn_c.reshape(n_layers, 1, gw)
    b_sb = jnp.broadcast_to(jnp.swapaxes(b_s, 1, 2)[..., None], b_s.shape[:1] + (b_s.shape[2], CHUNK, c_ch))
    cache = [t.reshape(t.shape[0], t.shape[1], n_past, -1)
             for t in (cache_ga_k, cache_ga_v, cache_sw_k, cache_sw_v)]

    ctx_row = lambda i, bm: 0
    lat_row = lambda i, bm: 1 + (i * bm) // s_lat

    def layer(x, l, latent):
        row_of = lat_row if latent else ctx_row
        s_len = s_lat if latent else s_ctx
        h = _norm_mod(x, g_mix, mod, l, 1, 0, row_of)
        p = _in_proj(h, w_in_b, l)
        if latent:
            oa = _attn_lat(p, a_col, a_kv, n_heads // a_kv, s_len, cache[0], cache[1], l, cos,
                           sin_signed, qn_a[l], kn_a[l], None, None)
            od = _attn_lat(p, d_col, d_kv, n_heads // d_kv, s_len, cache[2], cache[3], l, cos,
                           sin_signed, qn_d[l], kn_d[l], sink_d[l], WINDOW)
            new_kv = None
        else:
            oa, ka, va = _attn_ctx(p, a_col, a_kv, n_heads // a_kv, s_len, qn_a[l], kn_a[l], None)
            od, kd, vd = _attn_ctx(p, d_col, d_kv, n_heads // d_kv, s_len, qn_d[l], kn_d[l], sink_d[l])
            new_kv = (ka, va, kd, vd)
        ob = _fourier(p, b_col, gw, b_ch, s_len)
        oc = _gating(p, u_col, v_col, gw, c_ch, gn, w_s, b_sb, l)
        x = _out_proj((oa, ob, oc, od), w_o_b, x, mod, l, 2, row_of)
        h2 = _norm_mod(x, g_ffn, mod, l, 4, 3, row_of)
        act = _ffn_up(h2, w_up_b, conv_w, conv_b, l, s_len)
        x = _ffn_down(act, w_down_b, x, mod, l, 5, row_of)
        return x, new_kv

    xc = x_prompt.reshape(n_ctx * s_ctx, d)
    xl = x_sample.reshape(n_lat * s_lat, d)
    new_kv = []
    for l in range(n_layers):
        xc, kv_l = layer(xc, l, False)
        new_kv.append(kv_l)
    for l in range(n_layers):
        xl, _ = layer(xl, l, True)

    def stack(idx, n_kv):
        return jnp.stack([kv_l[idx].reshape(n_ctx, s_ctx, n_kv, HEAD_DIM) for kv_l in new_kv], axis=1)

    return (xc.reshape(n_ctx, s_ctx, d), xl.reshape(n_lat, s_lat, d),
            stack(0, a_kv), stack(1, a_kv), stack(2, d_kv), stack(3, d_kv))
```

```python
import functools
import math

import jax
import jax.numpy as jnp
from jax import lax
from jax.experimental import pallas as pl
from jax.experimental.pallas import tpu as pltpu

F32 = jnp.float32
BF16 = jnp.bfloat16

HEAD_DIM = 128
GRID_W = 64
CHUNK = 128
WINDOW = 128
ROPE_THETA = 10000.0
EPS = 1e-6
N_MIXERS = 4
FOURIER_GROUPS = 8
N_MOD = 6
MOD_ROWS = 16

V7X_VMEM_BYTES = 64 * 1024 * 1024
VMEM_LIMIT = V7X_VMEM_BYTES - 8 * 1024 * 1024

TILE_M = 1024
TILE_N = 1024
TILE_N_OUT = 512
TILE_FF = 512
TILE_K_DOWN = 2048
SIDE_COLS = 4096
BF16_ROWS = 16
NORM_ROWS = 512
ATTN_Q_BLOCK = 256
LOG2E = math.log2(math.e)
MASKED = -1e30


def _params(*sem):
    return pltpu.CompilerParams(dimension_semantics=sem, vmem_limit_bytes=VMEM_LIMIT)


def _dot(a, b):
    return jnp.dot(a, b, preferred_element_type=F32)


def _rms(x, gain):
    ms = jnp.mean(x * x, axis=-1, keepdims=True)
    return (x * lax.rsqrt(ms + EPS)) * gain


def _silu(x):
    return x * (1.0 / (1.0 + jnp.exp(-x)))


def _gelu_tanh(x):
    c = math.sqrt(2.0 / math.pi)
    return x * (0.5 * (1.0 + jnp.tanh(c * (x + 0.044715 * (x * x * x)))))


def _ada_kernel(c_ref, w_ref, b_ref, o_ref):
    a = _silu(c_ref[...]).astype(BF16)
    o_ref[...] = _dot(a, w_ref[...].astype(BF16)) + b_ref[...]


def _ada_mod(cond, w_ada, b_ada):
    n_layers, d, n = w_ada.shape
    tn = TILE_N
    return pl.pallas_call(
        _ada_kernel,
        grid=(n_layers, n // tn),
        in_specs=[
            pl.BlockSpec((MOD_ROWS, d), lambda l, j: (0, 0)),
            pl.BlockSpec((None, d, tn), lambda l, j: (l, 0, j)),
            pl.BlockSpec((None, 1, tn), lambda l, j: (l, 0, j)),
        ],
        out_specs=pl.BlockSpec((None, MOD_ROWS, tn), lambda l, j: (l, 0, j)),
        out_shape=jax.ShapeDtypeStruct((n_layers, MOD_ROWS, n), F32),
        compiler_params=_params("parallel", "parallel"),
        name="ada_mod",
    )(cond, w_ada, b_ada.reshape(n_layers, 1, n))


def _mod_spec(layer, chunk, row_of, width, col_of):
    return pl.BlockSpec(
        (None, None, None, 1, width),
        lambda *g: (layer, row_of(*g), chunk, 0, col_of(*g)))


def _norm_mod_kernel(x_ref, g_ref, sc_ref, sh_ref, o_ref):
    y = _rms(x_ref[...], g_ref[...])
    o_ref[...] = (y * (1.0 + sc_ref[...]) + sh_ref[...]).astype(o_ref.dtype)


def _norm_mod(x, gain, mod, layer, sc_chunk, sh_chunk, row_of_tile):
    m, d = x.shape
    bm = NORM_ROWS
    row_of = lambda i: row_of_tile(i, bm)
    zero = lambda i: 0
    return pl.pallas_call(
        _norm_mod_kernel,
        grid=(m // bm,),
        in_specs=[
            pl.BlockSpec((bm, d), lambda i: (i, 0)),
            pl.BlockSpec((None, 1, d), lambda i: (layer, 0, 0)),
            _mod_spec(layer, sc_chunk, row_of, d, zero),
            _mod_spec(layer, sh_chunk, row_of, d, zero),
        ],
        out_specs=pl.BlockSpec((bm, d), lambda i: (i, 0)),
        out_shape=jax.ShapeDtypeStruct((m, d), BF16),
        compiler_params=_params("parallel"),
        name="norm_mod",
    )(x, gain.reshape(gain.shape[0], 1, d), mod, mod)


def _proj_kernel(x_ref, w_ref, o_ref):
    o_ref[...] = _dot(x_ref[...], w_ref[...]).astype(o_ref.dtype)


def _in_proj(h, w):
    m, k = h.shape
    n = w.shape[-1]
    return pl.pallas_call(
        _proj_kernel,
        grid=(m // TILE_M, n // TILE_N),
        in_specs=[
            pl.BlockSpec((TILE_M, k), lambda i, j: (i, 0)),
            pl.BlockSpec((k, TILE_N), lambda i, j: (0, j)),
        ],
        out_specs=pl.BlockSpec((TILE_M, TILE_N), lambda i, j: (i, j)),
        out_shape=jax.ShapeDtypeStruct((m, n), F32),
        compiler_params=_params("parallel", "parallel"),
        name="in_proj",
    )(h, w)


def _out_proj_kernel(a_ref, b_ref, c_ref, d_ref, w_ref, x_ref, gt_ref, o_ref):
    kw = a_ref.shape[1]
    acc = _dot(a_ref[...], w_ref[0 * kw:1 * kw, :])
    acc += _dot(b_ref[...], w_ref[1 * kw:2 * kw, :])
    acc += _dot(c_ref[...], w_ref[2 * kw:3 * kw, :])
    acc += _dot(d_ref[...], w_ref[3 * kw:4 * kw, :])
    o_ref[...] = x_ref[...] + gt_ref[...] * acc


def _out_proj(mixed, w, x, mod, layer, gt_chunk, row_of_tile):
    m, d = x.shape
    kw = mixed[0].shape[1]
    tn = TILE_N_OUT
    row_of = lambda i, j: row_of_tile(i, TILE_M)
    mix_spec = pl.BlockSpec((TILE_M, kw), lambda i, j: (i, 0))
    return pl.pallas_call(
        _out_proj_kernel,
        grid=(m // TILE_M, d // tn),
        in_specs=[mix_spec] * N_MIXERS + [
            pl.BlockSpec((N_MIXERS * kw, tn), lambda i, j: (0, j)),
            pl.BlockSpec((TILE_M, tn), lambda i, j: (i, j)),
            _mod_spec(layer, gt_chunk, row_of, tn, lambda i, j: j),
        ],
        out_specs=pl.BlockSpec((TILE_M, tn), lambda i, j: (i, j)),
        out_shape=jax.ShapeDtypeStruct((m, d), F32),
        compiler_params=_params("parallel", "parallel"),
        name="out_proj",
    )(*mixed, w, x, mod)


def _ffn_up_kernel(h_ref, wg_ref, wu_ref, cwg_ref, cwu_ref, cbg_ref, cbu_ref, *rest, seq_len):
    n_side = len(rest) // 2
    o_ref = rest[n_side]
    for src_ref, dst_ref in zip(rest[:n_side], rest[n_side + 1:]):
        dst_ref[...] = src_ref[...].astype(dst_ref.dtype)
    h = h_ref[...]
    rows = h.shape[0]
    pos = lax.rem(lax.broadcasted_iota(jnp.int32, (rows, 1), 0), seq_len)
    first = pos == 0
    last = pos == seq_len - 1

    def conv(z, cw_ref, cb_ref):
        prev = jnp.where(first, 0.0, pltpu.roll(z, 1, 0))
        nxt = jnp.where(last, 0.0, pltpu.roll(z, rows - 1, 0))
        return prev * cw_ref[0:1, :] + z * cw_ref[1:2, :] + nxt * cw_ref[2:3, :] + cb_ref[...]

    gate = conv(_dot(h, wg_ref[...]), cwg_ref, cbg_ref)
    up = conv(_dot(h, wu_ref[...]), cwu_ref, cbu_ref)
    o_ref[...] = (_silu(gate) * up).astype(o_ref.dtype)


def _ffn_up(h, w_up, conv_w, conv_b, layer, seq_len, side=()):
    m, k = h.shape
    d_ff = w_up.shape[-1] // 2
    tn = TILE_FF
    nj = d_ff // tn
    ni = m // TILE_M
    assert TILE_M % seq_len == 0
    w_spec = lambda off: pl.BlockSpec((k, tn), lambda i, j: (0, off + j))
    cw_spec = lambda off: pl.BlockSpec((None, conv_w.shape[1], tn), lambda i, j: (layer, 0, off + j))
    cb_spec = lambda off: pl.BlockSpec((None, 1, tn), lambda i, j: (layer, 0, off + j))
    conv_b3 = conv_b.reshape(conv_b.shape[0], 1, 2 * d_ff)
    side_in, side_out, side_shape = [], [], []
    for w_src, src_layer in side:
        _, r, c = w_src.shape
        cb = math.gcd(c // SIDE_COLS, ni * nj)
        rb = (ni * nj) // cb
        assert r % rb == 0 and (r // rb) % BF16_ROWS == 0 and c % cb == 0
        block = (r // rb, c // cb)
        at = lambda i, j, cb=cb: ((i * nj + j) // cb, (i * nj + j) % cb)
        side_in.append(pl.BlockSpec((None,) + block, lambda i, j, at=at, sl=src_layer: (sl,) + at(i, j)))
        side_out.append(pl.BlockSpec(block, at))
        side_shape.append(jax.ShapeDtypeStruct((r, c), BF16))
    out = pl.pallas_call(
        functools.partial(_ffn_up_kernel, seq_len=seq_len),
        grid=(ni, nj),
        in_specs=[
            pl.BlockSpec((TILE_M, k), lambda i, j: (i, 0)),
            w_spec(0), w_spec(nj), cw_spec(0), cw_spec(nj), cb_spec(0), cb_spec(nj),
        ] + side_in,
        out_specs=[pl.BlockSpec((TILE_M, tn), lambda i, j: (i, j))] + side_out,
        out_shape=[jax.ShapeDtypeStruct((m, d_ff), BF16)] + side_shape,
        compiler_params=_params("parallel", "parallel"),
        name="ffn_up",
    )(h, w_up, w_up, conv_w, conv_w, conv_b3, conv_b3, *[w_src for w_src, _ in side])
    return out[0], out[1:]


def _ffn_down_kernel(a_ref, w_ref, x_ref, gt_ref, o_ref, acc_ref):
    k = pl.program_id(2)
    last = pl.num_programs(2) - 1

    @pl.when(k == 0)
    def _():
        acc_ref[...] = _dot(a_ref[...], w_ref[...])

    @pl.when((k > 0) & (k < last))
    def _():
        acc_ref[...] += _dot(a_ref[...], w_ref[...])

    @pl.when(k == last)
    def _():
        o_ref[...] = x_ref[...] + gt_ref[...] * (acc_ref[...] + _dot(a_ref[...], w_ref[...]))


def _ffn_down(act, w, x, mod, layer, gt_chunk, row_of_tile):
    m, d = x.shape
    d_ff = act.shape[1]
    tk = TILE_K_DOWN
    row_of = lambda i, j, k: row_of_tile(i, TILE_M)
    return pl.pallas_call(
        _ffn_down_kernel,
        grid=(m // TILE_M, d // TILE_N, d_ff // tk),
        in_specs=[
            pl.BlockSpec((TILE_M, tk), lambda i, j, k: (i, k)),
            pl.BlockSpec((tk, TILE_N), lambda i, j, k: (k, j)),
            pl.BlockSpec((TILE_M, TILE_N), lambda i, j, k: (i, j)),
            _mod_spec(layer, gt_chunk, row_of, TILE_N, lambda i, j, k: j),
        ],
        out_specs=pl.BlockSpec((TILE_M, TILE_N), lambda i, j, k: (i, j)),
        out_shape=jax.ShapeDtypeStruct((m, d), F32),
        scratch_shapes=[pltpu.VMEM((TILE_M, TILE_N), F32)],
        compiler_params=_params("parallel", "parallel", "arbitrary"),
        name="ffn_down",
    )(act, w, x, mod)


def _softmax_pv(q, kt, v_ones, bias, sink):
    d = q.shape[1]
    scale = d ** -0.5
    s = _dot(q, kt)
    if bias is not None:
        s = s + bias
    m = jnp.max(s, axis=-1, keepdims=True) * scale
    if sink is not None:
        m = jnp.maximum(m, sink)
    m2 = m * LOG2E
    e = jnp.exp2(s * (scale * LOG2E) - m2).astype(BF16)
    ov = _dot(e, v_ones)
    den = ov[:, d:]
    if sink is not None:
        den = den + jnp.exp2(sink * LOG2E - m2)
    return ov[:, :d] / den


def _with_ones(v):
    return jnp.concatenate([v, jnp.ones_like(v)], axis=1)


def _attn_ctx_kernel(*refs, n_kv, groups, has_sink):
    if has_sink:
        qkv_ref, qn_ref, kn_ref, sink_ref, o_ref, ko_ref, vo_ref = refs
    else:
        qkv_ref, qn_ref, kn_ref, o_ref, ko_ref, vo_ref = refs
    qn = qn_ref[...]
    kn = kn_ref[...]
    k0 = n_kv * groups * HEAD_DIM
    v0 = k0 + n_kv * HEAD_DIM
    for h in range(n_kv):
        lanes = slice(h * HEAD_DIM, (h + 1) * HEAD_DIM)
        k = _rms(qkv_ref[:, k0 + h * HEAD_DIM:k0 + (h + 1) * HEAD_DIM], kn)
        v = qkv_ref[:, v0 + h * HEAD_DIM:v0 + (h + 1) * HEAD_DIM]
        ko_ref[:, lanes] = k
        vo_ref[:, lanes] = v
        kt = k.T.astype(BF16)
        v_ones = _with_ones(v.astype(BF16))
        for g in range(groups):
            head = h * groups + g
            cols = slice(head * HEAD_DIM, (head + 1) * HEAD_DIM)
            q = _rms(qkv_ref[:, cols], qn).astype(BF16)
            sink = sink_ref[head] if has_sink else None
            o_ref[:, cols] = _softmax_pv(q, kt, v_ones, None, sink).astype(o_ref.dtype)


def _attn_ctx(p, col0, n_kv, groups, s_len, qn, kn, sink):
    m = p.shape[0]
    qw = n_kv * groups * HEAD_DIM
    width = qw + 2 * n_kv * HEAD_DIM
    assert col0 % width == 0
    has_sink = sink is not None
    vec = pl.BlockSpec((1, HEAD_DIM), lambda b: (0, 0))
    in_specs = [pl.BlockSpec((s_len, width), lambda b: (b, col0 // width)), vec, vec]
    args = [p, qn.reshape(1, HEAD_DIM), kn.reshape(1, HEAD_DIM)]
    if has_sink:
        in_specs.append(pl.BlockSpec(memory_space=pltpu.SMEM))
        args.append(sink)
    kv_spec = pl.BlockSpec((s_len, n_kv * HEAD_DIM), lambda b: (b, 0))
    kv_shape = jax.ShapeDtypeStruct((m, n_kv * HEAD_DIM), F32)
    return pl.pallas_call(
        functools.partial(_attn_ctx_kernel, n_kv=n_kv, groups=groups, has_sink=has_sink),
        grid=(m // s_len,),
        in_specs=in_specs,
        out_specs=[pl.BlockSpec((s_len, qw), lambda b: (b, 0)), kv_spec, kv_spec],
        out_shape=[jax.ShapeDtypeStruct((m, qw), BF16), kv_shape, kv_shape],
        compiler_params=_params("parallel"),
        name="attn_ctx",
    )(*args)


def _rope(x, cos, sin_signed):
    lane = lax.broadcasted_iota(jnp.int32, x.shape, 1)
    partner = jnp.where((lane & (HEAD_DIM // 4)) == 0,
                        pltpu.roll(x, HEAD_DIM - HEAD_DIM // 4, 1),
                        pltpu.roll(x, HEAD_DIM // 4, 1))
    return x * cos + partner * sin_signed


def _attn_lat_kernel(*refs, groups, has_sink, window, n_past, q_block):
    if has_sink:
        (q_ref, k_ref, v_ref, kc_ref, vc_ref, cos_ref, sin_ref, qn_ref, kn_ref, sink_ref,
         o_ref) = refs
    else:
        q_ref, k_ref, v_ref, kc_ref, vc_ref, cos_ref, sin_ref, qn_ref, kn_ref, o_ref = refs
    s_len = q_ref.shape[0]
    kv = pl.program_id(1)
    qn = qn_ref[...]
    k = _rope(_rms(k_ref[...], kn_ref[...]), cos_ref[...], sin_ref[...])
    kt = jnp.concatenate([kc_ref[...].T, k.T], axis=1).astype(BF16)
    v_ones = _with_ones(jnp.concatenate([vc_ref[...], v_ref[...]], axis=0).astype(BF16))
    biases = {}
    for q0 in range(0, s_len, q_block):
        rows = slice(q0, q0 + q_block)
        if window is None:
            kt_blk, v_blk, bias = kt, v_ones, None
        else:
            span = q_block + 2 * window
            start = min(max(q0 - window, 0), s_len - span)
            band = slice(n_past + start, n_past + start + span)
            kt_blk = jnp.concatenate([kt[:, :n_past], kt[:, band]], axis=1)
            v_blk = jnp.concatenate([v_ones[:n_past], v_ones[band]], axis=0)
            if q0 - start not in biases:
                col = lax.broadcasted_iota(jnp.int32, (q_block, n_past + span), 1)
                row = lax.broadcasted_iota(jnp.int32, (q_block, n_past + span), 0)
                dist = jnp.abs((row + (q0 - start)) - (col - n_past))
                biases[q0 - start] = jnp.where((col < n_past) | (dist <= window), 0.0, MASKED)
            bias = biases[q0 - start]
        cos = cos_ref[rows, :]
        sin = sin_ref[rows, :]
        for g in range(groups):
            cols = slice(g * HEAD_DIM, (g + 1) * HEAD_DIM)
            q = _rope(_rms(q_ref[rows, cols], qn), cos, sin).astype(BF16)
            sink = sink_ref[kv * groups + g] if has_sink else None
            o_ref[rows, cols] = _softmax_pv(q, kt_blk, v_blk, bias, sink).astype(o_ref.dtype)


def _attn_lat(p, col0, n_kv, groups, s_len, cache_k, cache_v, layer, cos, sin_signed,
              qn, kn, sink, window):
    m = p.shape[0]
    qw = groups * HEAD_DIM
    n_past = cache_k.shape[2]
    k0 = (col0 + n_kv * qw) // HEAD_DIM
    v0 = k0 + n_kv
    has_sink = sink is not None
    vec = pl.BlockSpec((1, HEAD_DIM), lambda b, h: (0, 0))
    cache_spec = pl.BlockSpec((None, None, n_past, HEAD_DIM), lambda b, h: (b, layer, 0, h))
    table = pl.BlockSpec((s_len, HEAD_DIM), lambda b, h: (0, 0))
    in_specs = [
        pl.BlockSpec((s_len, qw), lambda b, h: (b, col0 // qw + h)),
        pl.BlockSpec((s_len, HEAD_DIM), lambda b, h: (b, k0 + h)),
        pl.BlockSpec((s_len, HEAD_DIM), lambda b, h: (b, v0 + h)),
        cache_spec, cache_spec, table, table, vec, vec,
    ]
    args = [p, p, p, cache_k, cache_v, cos, sin_signed,
            qn.reshape(1, HEAD_DIM), kn.reshape(1, HEAD_DIM)]
    if has_sink:
        in_specs.append(pl.BlockSpec(memory_space=pltpu.SMEM))
        args.append(sink)
    return pl.pallas_call(
        functools.partial(_attn_lat_kernel, groups=groups, has_sink=has_sink, window=window,
                          n_past=n_past, q_block=ATTN_Q_BLOCK),
        grid=(m // s_len, n_kv),
        in_specs=in_specs,
        out_specs=pl.BlockSpec((s_len, qw), lambda b, h: (b, h)),
        out_shape=jax.ShapeDtypeStruct((m, n_kv * qw), BF16),
        compiler_params=_params("parallel", "parallel"),
        name="attn_lat",
    )(*args)


def _rope_tables(s_len):
    quarter = HEAD_DIM // 4
    inv = ROPE_THETA ** (-jnp.arange(quarter, dtype=F32) / quarter)
    t = jnp.arange(s_len)
    ang_row = (t // GRID_W).astype(F32)[:, None] * inv[None, :]
    ang_col = (t % GRID_W).astype(F32)[:, None] * inv[None, :]
    cos = jnp.concatenate([jnp.cos(ang_row)] * 2 + [jnp.cos(ang_col)] * 2, axis=-1)
    sin = jnp.concatenate([-jnp.sin(ang_row), jnp.sin(ang_row),
                           -jnp.sin(ang_col), jnp.sin(ang_col)], axis=-1)
    return cos, sin


def _fourier_kernel(x_ref, cs_ref, f_ref, o_ref, *, ch, scale):
    x = x_ref[...].astype(BF16)
    cs = cs_ref[...]
    parts = [_dot(x[:, g * ch:(g + 1) * ch], cs) for g in range(x.shape[1] // ch)]
    y = jnp.concatenate(
        [jnp.concatenate([t[:, :ch] for t in parts], axis=1),
         jnp.concatenate([t[:, ch:] for t in parts], axis=1)], axis=0).astype(BF16)
    o_ref[...] = (_dot(f_ref[...], y) * scale).astype(o_ref.dtype)


def _dft_cos_sin(n):
    j = jnp.arange(n, dtype=jnp.int32)
    ang = ((j[:, None] * j[None, :]) % n).astype(F32) * (2.0 * math.pi / n)
    return jnp.cos(ang), jnp.sin(ang)


def _dft_mats(ch, s_len):
    cc, sc = _dft_cos_sin(ch)
    cp, sp = _dft_cos_sin(s_len)
    return (jnp.concatenate([cc, sc], axis=1).astype(BF16),
            jnp.concatenate([cp, -sp], axis=1).astype(BF16))


def _fourier(p, col0, width, ch, s_len, cs_mat, f_mat):
    m = p.shape[0]
    tw = 512
    return pl.pallas_call(
        functools.partial(_fourier_kernel, ch=ch, scale=1.0 / math.sqrt(s_len * ch)),
        grid=(m // s_len, width // tw),
        in_specs=[
            pl.BlockSpec((s_len, tw), lambda b, j: (b, col0 // tw + j)),
            pl.BlockSpec((ch, 2 * ch), lambda b, j: (0, 0)),
            pl.BlockSpec((s_len, 2 * s_len), lambda b, j: (0, 0)),
        ],
        out_specs=pl.BlockSpec((s_len, tw), lambda b, j: (b, j)),
        out_shape=jax.ShapeDtypeStruct((m, width), BF16),
        compiler_params=_params("parallel", "parallel"),
        name="fourier",
    )(p, cs_mat, f_mat)


def _gating_kernel(u_ref, v_ref, gn_ref, w_ref, b_ref, o_ref, *, ch):
    rows = u_ref.shape[0]
    n_chunks = rows // CHUNK
    u = _gelu_tanh(u_ref[...])
    v = _gelu_tanh(v_ref[...])
    for g in range(u.shape[1] // ch):
        lanes = slice(g * ch, (g + 1) * ch)
        vn = _rms(v[:, lanes], gn_ref[:, lanes]).astype(BF16)
        rhs = jnp.concatenate([vn[c * CHUNK:(c + 1) * CHUNK] for c in range(n_chunks)], axis=1)
        sv = _dot(w_ref[g].astype(BF16), rhs)
        bias = b_ref[g]
        for c in range(n_chunks):
            rs = slice(c * CHUNK, (c + 1) * CHUNK)
            o_ref[rs, lanes] = (u[rs, lanes] * (sv[:, c * ch:(c + 1) * ch] + bias)).astype(o_ref.dtype)


def _gating(p, u_col0, v_col0, width, ch, gn, w_s, b_s, layer):
    m = p.shape[0]
    tw = 512
    rows = 512
    gpt = tw // ch
    return pl.pallas_call(
        functools.partial(_gating_kernel, ch=ch),
        grid=(m // rows, width // tw),
        in_specs=[
            pl.BlockSpec((rows, tw), lambda i, j: (i, u_col0 // tw + j)),
            pl.BlockSpec((rows, tw), lambda i, j: (i, v_col0 // tw + j)),
            pl.BlockSpec((None, 1, tw), lambda i, j: (layer, 0, j)),
            pl.BlockSpec((None, gpt, CHUNK, CHUNK), lambda i, j: (layer, j, 0, 0)),
            pl.BlockSpec((None, gpt, CHUNK, ch), lambda i, j: (layer, j, 0, 0)),
        ],
        out_specs=pl.BlockSpec((rows, tw), lambda i, j: (i, j)),
        out_shape=jax.ShapeDtypeStruct((m, width), BF16),
        compiler_params=_params("parallel", "parallel"),
        name="spatial_gating",
    )(p, p, gn, w_s, b_s)


def kernel(x_prompt, x_sample, c, cache_ga_k, cache_ga_v, cache_sw_k, cache_sw_v, c_ctx, w_ada, b_ada, g_mix, w_in, qn_a, kn_a, qn_d, kn_d, sink_d, gn_c, w_s, b_s, w_o, g_ffn, w_up, conv_w, conv_b, w_down):
    n_layers = w_ada.shape[0]
    n_ctx, s_ctx, d = x_prompt.shape
    n_lat, s_lat, _ = x_sample.shape
    gw = d // N_MIXERS
    n_heads = gw // HEAD_DIM
    a_kv = cache_ga_k.shape[3]
    d_kv = cache_sw_k.shape[3]
    b_ch = gw // FOURIER_GROUPS
    c_ch = gn_c.shape[2]
    n_past = cache_ga_k.shape[2]
    assert n_lat + 1 <= MOD_ROWS and s_lat % GRID_W == 0

    a_col = 0
    b_col = a_col + gw + 2 * a_kv * HEAD_DIM
    u_col = b_col + gw
    v_col = u_col + gw
    d_col = v_col + gw

    w_in_b = [w_in[l].astype(BF16) for l in range(n_layers)]
    w_o_b = [w_o[l].astype(BF16) for l in range(n_layers)]
    w_up_b = [w_up[0].astype(BF16)] + [None] * (n_layers - 1)
    w_down_b = [None] * n_layers

    cond = jnp.concatenate([c_ctx[None, :], c], axis=0)
    cond = jnp.pad(cond, ((0, MOD_ROWS - cond.shape[0]), (0, 0)))
    mod = _ada_mod(cond, w_ada, b_ada).reshape(n_layers, MOD_ROWS, N_MOD, 1, d)

    cos, sin_signed = _rope_tables(s_lat)
    dft = {s: _dft_mats(b_ch, s) for s in {s_ctx, s_lat}}
    gn = gn_c.reshape(n_layers, 1, gw)
    b_sb = jnp.broadcast_to(jnp.swapaxes(b_s, 1, 2)[..., None], b_s.shape[:1] + (b_s.shape[2], CHUNK, c_ch))
    cache = [t.reshape(t.shape[0], t.shape[1], n_past, -1)
             for t in (cache_ga_k, cache_ga_v, cache_sw_k, cache_sw_v)]

    ctx_row = lambda i, bm: 0
    lat_row = lambda i, bm: 1 + (i * bm) // s_lat

    def layer(x, l, latent):
        row_of = lat_row if latent else ctx_row
        s_len = s_lat if latent else s_ctx
        h = _norm_mod(x, g_mix, mod, l, 1, 0, row_of)
        p = _in_proj(h, w_in_b[l])
        if latent:
            oa = _attn_lat(p, a_col, a_kv, n_heads // a_kv, s_len, cache[0], cache[1], l, cos,
                           sin_signed, qn_a[l], kn_a[l], None, None)
            od = _attn_lat(p, d_col, d_kv, n_heads // d_kv, s_len, cache[2], cache[3], l, cos,
                           sin_signed, qn_d[l], kn_d[l], sink_d[l], WINDOW)
            new_kv = None
        else:
            oa, ka, va = _attn_ctx(p, a_col, a_kv, n_heads // a_kv, s_len, qn_a[l], kn_a[l], None)
            od, kd, vd = _attn_ctx(p, d_col, d_kv, n_heads // d_kv, s_len, qn_d[l], kn_d[l], sink_d[l])
            new_kv = (ka, va, kd, vd)
        ob = _fourier(p, b_col, gw, b_ch, s_len, *dft[s_len])
        oc = _gating(p, u_col, v_col, gw, c_ch, gn, w_s, b_sb, l)
        x = _out_proj((oa, ob, oc, od), w_o_b[l], x, mod, l, 2, row_of)
        h2 = _norm_mod(x, g_ffn, mod, l, 4, 3, row_of)
        if w_down_b[l] is None:
            side = [(w_down, l)] + ([(w_up, l + 1)] if l + 1 < n_layers else [])
            act, rounded = _ffn_up(h2, w_up_b[l], conv_w, conv_b, l, s_len, side)
            w_down_b[l] = rounded[0]
            if l + 1 < n_layers:
                w_up_b[l + 1] = rounded[1]
        else:
            act, _ = _ffn_up(h2, w_up_b[l], conv_w, conv_b, l, s_len)
        x = _ffn_down(act, w_down_b[l], x, mod, l, 5, row_of)
        return x, new_kv

    xc = x_prompt.reshape(n_ctx * s_ctx, d)
    xl = x_sample.reshape(n_lat * s_lat, d)
    new_kv = []
    for l in range(n_layers):
        xc, kv_l = layer(xc, l, False)
        new_kv.append(kv_l)
    for l in range(n_layers):
        xl, _ = layer(xl, l, True)

    def stack(idx, n_kv):
        return jnp.stack([kv_l[idx].reshape(n_ctx, s_ctx, n_kv, HEAD_DIM) for kv_l in new_kv], axis=1)

    return (xc.reshape(n_ctx, s_ctx, d), xl.reshape(n_lat, s_lat, d),
            stack(0, a_kv), stack(1, a_kv), stack(2, d_kv), stack(3, d_kv))
```

```python
import functools
import math

import jax
import jax.numpy as jnp
import numpy as np
from jax import lax
from jax.experimental import pallas as pl
from jax.experimental.pallas import tpu as pltpu

F32 = jnp.float32
BF16 = jnp.bfloat16

HEAD_DIM = 128
GRID_W = 64
CHUNK = 128
WINDOW = 128
ROPE_THETA = 10000.0
EPS = 1e-6
N_MIXERS = 4
FOURIER_GROUPS = 8
N_MOD = 6
MOD_ROWS = 16

V7X_VMEM_BYTES = 64 * 1024 * 1024
VMEM_LIMIT = V7X_VMEM_BYTES - 8 * 1024 * 1024

TILE_M = 1024
TILE_N = 1024
TILE_N_OUT = 512
TILE_FF = 512
FFN_ROW_PIECES = 8
TILE_K_DOWN = 2560
SIDE_COLS = 4096
BF16_ROWS = 16
NORM_ROWS = 512
ATTN_Q_BLOCK = 256
LOG2E = math.log2(math.e)
MASKED = -1e30


def _params(*sem):
    return pltpu.CompilerParams(dimension_semantics=sem, vmem_limit_bytes=VMEM_LIMIT)


def _dot(a, b):
    return jnp.dot(a, b, preferred_element_type=F32)


def _rms(x, gain):
    ms = jnp.mean(x * x, axis=-1, keepdims=True)
    return (x * lax.rsqrt(ms + EPS)) * gain


def _silu(x):
    return x * (1.0 / (1.0 + jnp.exp(-x)))


def _gelu_tanh(x):
    c = math.sqrt(2.0 / math.pi)
    return x * (0.5 * (1.0 + jnp.tanh(c * (x + 0.044715 * (x * x * x)))))


def _ada_kernel(c_ref, w_ref, b_ref, o_ref):
    a = _silu(c_ref[...]).astype(BF16)
    o_ref[...] = _dot(a, w_ref[...].astype(BF16)) + b_ref[...]


def _ada_mod(cond, w_ada, b_ada):
    n_layers, d, n = w_ada.shape
    tn = TILE_N
    return pl.pallas_call(
        _ada_kernel,
        grid=(n_layers, n // tn),
        in_specs=[
            pl.BlockSpec((MOD_ROWS, d), lambda l, j: (0, 0)),
            pl.BlockSpec((None, d, tn), lambda l, j: (l, 0, j)),
            pl.BlockSpec((None, 1, tn), lambda l, j: (l, 0, j)),
        ],
        out_specs=pl.BlockSpec((None, MOD_ROWS, tn), lambda l, j: (l, 0, j)),
        out_shape=jax.ShapeDtypeStruct((n_layers, MOD_ROWS, n), F32),
        compiler_params=_params("parallel", "parallel"),
        name="ada_mod",
    )(cond, w_ada, b_ada.reshape(n_layers, 1, n))


def _mod_spec(layer, chunk, row_of, width, col_of):
    return pl.BlockSpec(
        (None, None, None, 1, width),
        lambda *g: (layer, row_of(*g), chunk, 0, col_of(*g)))


def _norm_mod_kernel(x_ref, g_ref, sc_ref, sh_ref, o_ref):
    y = _rms(x_ref[...], g_ref[...])
    o_ref[...] = (y * (1.0 + sc_ref[...]) + sh_ref[...]).astype(o_ref.dtype)


def _norm_mod(x, gain, mod, layer, sc_chunk, sh_chunk, row_of_tile):
    m, d = x.shape
    bm = NORM_ROWS
    row_of = lambda i: row_of_tile(i, bm)
    zero = lambda i: 0
    return pl.pallas_call(
        _norm_mod_kernel,
        grid=(m // bm,),
        in_specs=[
            pl.BlockSpec((bm, d), lambda i: (i, 0)),
            pl.BlockSpec((None, 1, d), lambda i: (layer, 0, 0)),
            _mod_spec(layer, sc_chunk, row_of, d, zero),
            _mod_spec(layer, sh_chunk, row_of, d, zero),
        ],
        out_specs=pl.BlockSpec((bm, d), lambda i: (i, 0)),
        out_shape=jax.ShapeDtypeStruct((m, d), BF16),
        compiler_params=_params("parallel"),
        name="norm_mod",
    )(x, gain.reshape(gain.shape[0], 1, d), mod, mod)


def _row_pieces(rows, pieces):
    step = rows // pieces
    return [slice(r, r + step) for r in range(0, rows, step)]


def _proj_kernel(x_ref, w_ref, o_ref):
    o_ref[...] = _dot(x_ref[...], w_ref[...]).astype(o_ref.dtype)


def _weight_spec(w, block, index):
    if isinstance(w, tuple):
        layer = w[1]
        return pl.BlockSpec((None,) + block, lambda *g: (layer,) + index(*g))
    return pl.BlockSpec(block, index)


def _weight_array(w):
    return w[0] if isinstance(w, tuple) else w


def _in_proj(h, w):
    m, k = h.shape
    n = _weight_array(w).shape[-1]
    return pl.pallas_call(
        _proj_kernel,
        grid=(m // TILE_M, n // TILE_N),
        in_specs=[
            pl.BlockSpec((TILE_M, k), lambda i, j: (i, 0)),
            _weight_spec(w, (k, TILE_N), lambda i, j: (0, j)),
        ],
        out_specs=pl.BlockSpec((TILE_M, TILE_N), lambda i, j: (i, j)),
        out_shape=jax.ShapeDtypeStruct((m, n), F32),
        compiler_params=_params("parallel", "parallel"),
        name="in_proj",
    )(h, _weight_array(w))


def _out_proj_kernel(a_ref, b_ref, c_ref, d_ref, w_ref, x_ref, gt_ref, o_ref):
    kw = a_ref.shape[1]
    acc = _dot(a_ref[...], w_ref[0 * kw:1 * kw, :])
    acc += _dot(b_ref[...], w_ref[1 * kw:2 * kw, :])
    acc += _dot(c_ref[...], w_ref[2 * kw:3 * kw, :])
    acc += _dot(d_ref[...], w_ref[3 * kw:4 * kw, :])
    o_ref[...] = x_ref[...] + gt_ref[...] * acc


def _out_proj(mixed, w, x, mod, layer, gt_chunk, row_of_tile):
    m, d = x.shape
    kw = mixed[0].shape[1]
    tn = TILE_N_OUT
    row_of = lambda i, j: row_of_tile(i, TILE_M)
    mix_spec = pl.BlockSpec((TILE_M, kw), lambda i, j: (i, 0))
    return pl.pallas_call(
        _out_proj_kernel,
        grid=(m // TILE_M, d // tn),
        in_specs=[mix_spec] * N_MIXERS + [
            _weight_spec(w, (N_MIXERS * kw, tn), lambda i, j: (0, j)),
            pl.BlockSpec((TILE_M, tn), lambda i, j: (i, j)),
            _mod_spec(layer, gt_chunk, row_of, tn, lambda i, j: j),
        ],
        out_specs=pl.BlockSpec((TILE_M, tn), lambda i, j: (i, j)),
        out_shape=jax.ShapeDtypeStruct((m, d), F32),
        compiler_params=_params("parallel", "parallel"),
        name="out_proj",
    )(*mixed, _weight_array(w), x, mod)


def _ffn_up_kernel(h_ref, wg_ref, wu_ref, cwg_ref, cwu_ref, cbg_ref, cbu_ref, *rest, seq_len):
    n_side = len(rest) // 2
    o_ref = rest[n_side]
    for src_ref, dst_ref in zip(rest[:n_side], rest[n_side + 1:]):
        dst_ref[...] = src_ref[...].astype(dst_ref.dtype)
    h = h_ref[...]
    rows = h.shape[0]
    pos = lax.rem(lax.broadcasted_iota(jnp.int32, (rows, 1), 0), seq_len)
    first = pos == 0
    last = pos == seq_len - 1

    def conv(z, cw_ref, cb_ref):
        prev = jnp.where(first, 0.0, pltpu.roll(z, 1, 0))
        nxt = jnp.where(last, 0.0, pltpu.roll(z, rows - 1, 0))
        return prev * cw_ref[0:1, :] + z * cw_ref[1:2, :] + nxt * cw_ref[2:3, :] + cb_ref[...]

    def rows_dot(w_ref):
        w = w_ref[...]
        return jnp.concatenate([_dot(h[r], w) for r in _row_pieces(rows, FFN_ROW_PIECES)], axis=0)

    gate = conv(rows_dot(wg_ref), cwg_ref, cbg_ref)
    up = conv(rows_dot(wu_ref), cwu_ref, cbu_ref)
    o_ref[...] = (_silu(gate) * up).astype(o_ref.dtype)


def _ffn_up(h, w_up, conv_w, conv_b, layer, seq_len, side=()):
    m, k = h.shape
    w_up_arr = _weight_array(w_up)
    d_ff = w_up_arr.shape[-1] // 2
    tn = TILE_FF
    nj = d_ff // tn
    ni = m // TILE_M
    assert TILE_M % seq_len == 0
    w_spec = lambda off: _weight_spec(w_up, (k, tn), lambda i, j: (0, off + j))
    cw_spec = lambda off: pl.BlockSpec((None, conv_w.shape[1], tn), lambda i, j: (layer, 0, off + j))
    cb_spec = lambda off: pl.BlockSpec((None, 1, tn), lambda i, j: (layer, 0, off + j))
    conv_b3 = conv_b.reshape(conv_b.shape[0], 1, 2 * d_ff)
    side_in, side_out, side_shape = [], [], []
    for w_src, src_layer in side:
        _, r, c = w_src.shape
        cb = math.gcd(c // SIDE_COLS, ni * nj)
        rb = (ni * nj) // cb
        assert r % rb == 0 and (r // rb) % BF16_ROWS == 0 and c % cb == 0
        block = (r // rb, c // cb)
        at = lambda i, j, cb=cb: ((i * nj + j) // cb, (i * nj + j) % cb)
        side_in.append(pl.BlockSpec((None,) + block, lambda i, j, at=at, sl=src_layer: (sl,) + at(i, j)))
        side_out.append(pl.BlockSpec(block, at))
        side_shape.append(jax.ShapeDtypeStruct((r, c), BF16))
    out = pl.pallas_call(
        functools.partial(_ffn_up_kernel, seq_len=seq_len),
        grid=(ni, nj),
        in_specs=[
            pl.BlockSpec((TILE_M, k), lambda i, j: (i, 0)),
            w_spec(0), w_spec(nj), cw_spec(0), cw_spec(nj), cb_spec(0), cb_spec(nj),
        ] + side_in,
        out_specs=[pl.BlockSpec((TILE_M, tn), lambda i, j: (i, j))] + side_out,
        out_shape=[jax.ShapeDtypeStruct((m, d_ff), BF16)] + side_shape,
        compiler_params=_params("parallel", "parallel"),
        name="ffn_up",
    )(h, w_up_arr, w_up_arr, conv_w, conv_w, conv_b3, conv_b3, *[w_src for w_src, _ in side])
    return out[0], out[1:]


def _ffn_down_kernel(a_ref, w_ref, x_ref, gt_ref, o_ref, acc_ref):
    k = pl.program_id(2)
    last = pl.num_programs(2) - 1

    @pl.when(k == 0)
    def _():
        acc_ref[...] = _dot(a_ref[...], w_ref[...])

    @pl.when((k > 0) & (k < last))
    def _():
        acc_ref[...] += _dot(a_ref[...], w_ref[...])

    @pl.when(k == last)
    def _():
        o_ref[...] = x_ref[...] + gt_ref[...] * (acc_ref[...] + _dot(a_ref[...], w_ref[...]))


def _ffn_down(act, w, x, mod, layer, gt_chunk, row_of_tile):
    m, d = x.shape
    d_ff = act.shape[1]
    tk = TILE_K_DOWN
    row_of = lambda i, j, k: row_of_tile(i, TILE_M)
    return pl.pallas_call(
        _ffn_down_kernel,
        grid=(m // TILE_M, d // TILE_N, d_ff // tk),
        in_specs=[
            pl.BlockSpec((TILE_M, tk), lambda i, j, k: (i, k)),
            _weight_spec(w, (tk, TILE_N), lambda i, j, k: (k, j)),
            pl.BlockSpec((TILE_M, TILE_N), lambda i, j, k: (i, j)),
            _mod_spec(layer, gt_chunk, row_of, TILE_N, lambda i, j, k: j),
        ],
        out_specs=pl.BlockSpec((TILE_M, TILE_N), lambda i, j, k: (i, j)),
        out_shape=jax.ShapeDtypeStruct((m, d), F32),
        scratch_shapes=[pltpu.VMEM((TILE_M, TILE_N), F32)],
        compiler_params=_params("parallel", "parallel", "arbitrary"),
        name="ffn_down",
    )(act, _weight_array(w), x, mod)


def _softmax_pv(q, kt, v_ones, bias, sink):
    d = q.shape[1]
    scale = d ** -0.5
    s = _dot(q, kt)
    if bias is not None:
        s = s + bias
    m = jnp.max(s, axis=-1, keepdims=True) * scale
    if sink is not None:
        m = jnp.maximum(m, sink)
    m2 = m * LOG2E
    e = jnp.exp2(s * (scale * LOG2E) - m2).astype(BF16)
    ov = _dot(e, v_ones)
    den = ov[:, d:]
    if sink is not None:
        den = den + jnp.exp2(sink * LOG2E - m2)
    return ov[:, :d] / den


def _with_ones(v):
    return jnp.concatenate([v, jnp.ones_like(v)], axis=1)


def _attn_ctx_kernel(*refs, n_kv, groups, has_sink):
    if has_sink:
        qkv_ref, qn_ref, kn_ref, sink_ref, o_ref, ko_ref, vo_ref = refs
    else:
        qkv_ref, qn_ref, kn_ref, o_ref, ko_ref, vo_ref = refs
    qn = qn_ref[...]
    kn = kn_ref[...]
    k0 = n_kv * groups * HEAD_DIM
    v0 = k0 + n_kv * HEAD_DIM
    for h in range(n_kv):
        lanes = slice(h * HEAD_DIM, (h + 1) * HEAD_DIM)
        k = _rms(qkv_ref[:, k0 + h * HEAD_DIM:k0 + (h + 1) * HEAD_DIM], kn)
        v = qkv_ref[:, v0 + h * HEAD_DIM:v0 + (h + 1) * HEAD_DIM]
        ko_ref[:, lanes] = k
        vo_ref[:, lanes] = v
        kt = k.T.astype(BF16)
        v_ones = _with_ones(v.astype(BF16))
        for g in range(groups):
            head = h * groups + g
            cols = slice(head * HEAD_DIM, (head + 1) * HEAD_DIM)
            q = _rms(qkv_ref[:, cols], qn).astype(BF16)
            sink = sink_ref[head] if has_sink else None
            o_ref[:, cols] = _softmax_pv(q, kt, v_ones, None, sink).astype(o_ref.dtype)


def _attn_ctx(p, col0, n_kv, groups, s_len, qn, kn, sink):
    m = p.shape[0]
    qw = n_kv * groups * HEAD_DIM
    width = qw + 2 * n_kv * HEAD_DIM
    assert col0 % width == 0
    has_sink = sink is not None
    vec = pl.BlockSpec((1, HEAD_DIM), lambda b: (0, 0))
    in_specs = [pl.BlockSpec((s_len, width), lambda b: (b, col0 // width)), vec, vec]
    args = [p, qn.reshape(1, HEAD_DIM), kn.reshape(1, HEAD_DIM)]
    if has_sink:
        in_specs.append(pl.BlockSpec(memory_space=pltpu.SMEM))
        args.append(sink)
    kv_spec = pl.BlockSpec((s_len, n_kv * HEAD_DIM), lambda b: (b, 0))
    kv_shape = jax.ShapeDtypeStruct((m, n_kv * HEAD_DIM), F32)
    return pl.pallas_call(
        functools.partial(_attn_ctx_kernel, n_kv=n_kv, groups=groups, has_sink=has_sink),
        grid=(m // s_len,),
        in_specs=in_specs,
        out_specs=[pl.BlockSpec((s_len, qw), lambda b: (b, 0)), kv_spec, kv_spec],
        out_shape=[jax.ShapeDtypeStruct((m, qw), BF16), kv_shape, kv_shape],
        compiler_params=_params("parallel"),
        name="attn_ctx",
    )(*args)


def _rope(x, cos, sin_signed):
    lane = lax.broadcasted_iota(jnp.int32, x.shape, 1)
    partner = jnp.where((lane & (HEAD_DIM // 4)) == 0,
                        pltpu.roll(x, HEAD_DIM - HEAD_DIM // 4, 1),
                        pltpu.roll(x, HEAD_DIM // 4, 1))
    return x * cos + partner * sin_signed


def _attn_lat_kernel(*refs, groups, has_sink, window, n_past, q_block):
    if has_sink:
        (q_ref, k_ref, v_ref, kc_ref, vc_ref, cos_ref, sin_ref, qn_ref, kn_ref, sink_ref,
         o_ref) = refs
    else:
        q_ref, k_ref, v_ref, kc_ref, vc_ref, cos_ref, sin_ref, qn_ref, kn_ref, o_ref = refs
    s_len = q_ref.shape[0]
    kv = pl.program_id(1)
    qn = qn_ref[...]
    k = _rope(_rms(k_ref[...], kn_ref[...]), cos_ref[...], sin_ref[...])
    kt = jnp.concatenate([kc_ref[...].T, k.T], axis=1).astype(BF16)
    v_ones = _with_ones(jnp.concatenate([vc_ref[...], v_ref[...]], axis=0).astype(BF16))
    biases = {}
    for q0 in range(0, s_len, q_block):
        rows = slice(q0, q0 + q_block)
        if window is None:
            kt_blk, v_blk, bias = kt, v_ones, None
        else:
            span = q_block + 2 * window
            start = min(max(q0 - window, 0), s_len - span)
            band = slice(n_past + start, n_past + start + span)
            kt_blk = jnp.concatenate([kt[:, :n_past], kt[:, band]], axis=1)
            v_blk = jnp.concatenate([v_ones[:n_past], v_ones[band]], axis=0)
            if q0 - start not in biases:
                col = lax.broadcasted_iota(jnp.int32, (q_block, n_past + span), 1)
                row = lax.broadcasted_iota(jnp.int32, (q_block, n_past + span), 0)
                dist = jnp.abs((row + (q0 - start)) - (col - n_past))
                biases[q0 - start] = jnp.where((col < n_past) | (dist <= window), 0.0, MASKED)
            bias = biases[q0 - start]
        cos = cos_ref[rows, :]
        sin = sin_ref[rows, :]
        for g in range(groups):
            cols = slice(g * HEAD_DIM, (g + 1) * HEAD_DIM)
            q = _rope(_rms(q_ref[rows, cols], qn), cos, sin).astype(BF16)
            sink = sink_ref[kv * groups + g] if has_sink else None
            o_ref[rows, cols] = _softmax_pv(q, kt_blk, v_blk, bias, sink).astype(o_ref.dtype)


def _attn_lat(p, col0, n_kv, groups, s_len, cache_k, cache_v, layer, cos, sin_signed,
              qn, kn, sink, window):
    m = p.shape[0]
    qw = groups * HEAD_DIM
    n_past = cache_k.shape[2]
    k0 = (col0 + n_kv * qw) // HEAD_DIM
    v0 = k0 + n_kv
    has_sink = sink is not None
    vec = pl.BlockSpec((1, HEAD_DIM), lambda b, h: (0, 0))
    cache_spec = pl.BlockSpec((None, None, n_past, HEAD_DIM), lambda b, h: (b, layer, 0, h))
    table = pl.BlockSpec((s_len, HEAD_DIM), lambda b, h: (0, 0))
    in_specs = [
        pl.BlockSpec((s_len, qw), lambda b, h: (b, col0 // qw + h)),
        pl.BlockSpec((s_len, HEAD_DIM), lambda b, h: (b, k0 + h)),
        pl.BlockSpec((s_len, HEAD_DIM), lambda b, h: (b, v0 + h)),
        cache_spec, cache_spec, table, table, vec, vec,
    ]
    args = [p, p, p, cache_k, cache_v, cos, sin_signed,
            qn.reshape(1, HEAD_DIM), kn.reshape(1, HEAD_DIM)]
    if has_sink:
        in_specs.append(pl.BlockSpec(memory_space=pltpu.SMEM))
        args.append(sink)
    return pl.pallas_call(
        functools.partial(_attn_lat_kernel, groups=groups, has_sink=has_sink, window=window,
                          n_past=n_past, q_block=ATTN_Q_BLOCK),
        grid=(m // s_len, n_kv),
        in_specs=in_specs,
        out_specs=pl.BlockSpec((s_len, qw), lambda b, h: (b, h)),
        out_shape=jax.ShapeDtypeStruct((m, n_kv * qw), BF16),
        compiler_params=_params("parallel", "parallel"),
        name="attn_lat",
    )(*args)


def _rope_tables(s_len):
    quarter = HEAD_DIM // 4
    inv = ROPE_THETA ** (-jnp.arange(quarter, dtype=F32) / quarter)
    t = jnp.arange(s_len)
    ang_row = (t // GRID_W).astype(F32)[:, None] * inv[None, :]
    ang_col = (t % GRID_W).astype(F32)[:, None] * inv[None, :]
    cos = jnp.concatenate([jnp.cos(ang_row)] * 2 + [jnp.cos(ang_col)] * 2, axis=-1)
    sin = jnp.concatenate([-jnp.sin(ang_row), jnp.sin(ang_row),
                           -jnp.sin(ang_col), jnp.sin(ang_col)], axis=-1)
    return cos, sin


def _fourier_kernel(x_ref, cs_ref, f_ref, o_ref, *, ch, scale):
    x = x_ref[...].astype(BF16)
    cs = cs_ref[...]
    parts = [_dot(x[:, g * ch:(g + 1) * ch], cs) for g in range(x.shape[1] // ch)]
    y = jnp.concatenate(
        [jnp.concatenate([t[:, :ch] for t in parts], axis=1),
         jnp.concatenate([t[:, ch:] for t in parts], axis=1)], axis=0).astype(BF16)
    o_ref[...] = (_dot(f_ref[...], y) * scale).astype(o_ref.dtype)


def _dft_cos_sin(n):
    j = np.arange(n, dtype=np.int64)
    ang = ((j[:, None] * j[None, :]) % n).astype(np.float64) * (2.0 * math.pi / n)
    return np.cos(ang).astype(np.float32), np.sin(ang).astype(np.float32)


def _dft_mats(ch, s_len):
    cc, sc = _dft_cos_sin(ch)
    cp, sp = _dft_cos_sin(s_len)
    return (jnp.asarray(np.concatenate([cc, sc], axis=1), BF16),
            jnp.asarray(np.concatenate([cp, -sp], axis=1), BF16))


def _fourier(p, col0, width, ch, s_len, cs_mat, f_mat):
    m = p.shape[0]
    tw = 512
    return pl.pallas_call(
        functools.partial(_fourier_kernel, ch=ch, scale=1.0 / math.sqrt(s_len * ch)),
        grid=(m // s_len, width // tw),
        in_specs=[
            pl.BlockSpec((s_len, tw), lambda b, j: (b, col0 // tw + j)),
            pl.BlockSpec((ch, 2 * ch), lambda b, j: (0, 0)),
            pl.BlockSpec((s_len, 2 * s_len), lambda b, j: (0, 0)),
        ],
        out_specs=pl.BlockSpec((s_len, tw), lambda b, j: (b, j)),
        out_shape=jax.ShapeDtypeStruct((m, width), BF16),
        compiler_params=_params("parallel", "parallel"),
        name="fourier",
    )(p, cs_mat, f_mat)


def _gating_kernel(u_ref, v_ref, gn_ref, w_ref, b_ref, o_ref, *, ch):
    rows = u_ref.shape[0]
    n_chunks = rows // CHUNK
    u = _gelu_tanh(u_ref[...])
    v = _gelu_tanh(v_ref[...])
    for g in range(u.shape[1] // ch):
        lanes = slice(g * ch, (g + 1) * ch)
        vn = _rms(v[:, lanes], gn_ref[:, lanes]).astype(BF16)
        rhs = jnp.concatenate([vn[c * CHUNK:(c + 1) * CHUNK] for c in range(n_chunks)], axis=1)
        sv = _dot(w_ref[g].astype(BF16), rhs)
        bias = b_ref[g]
        for c in range(n_chunks):
            rs = slice(c * CHUNK, (c + 1) * CHUNK)
            o_ref[rs, lanes] = (u[rs, lanes] * (sv[:, c * ch:(c + 1) * ch] + bias)).astype(o_ref.dtype)


def _gating(p, u_col0, v_col0, width, ch, gn, w_s, b_s, layer):
    m = p.shape[0]
    tw = 512
    rows = 512
    gpt = tw // ch
    return pl.pallas_call(
        functools.partial(_gating_kernel, ch=ch),
        grid=(m // rows, width // tw),
        in_specs=[
            pl.BlockSpec((rows, tw), lambda i, j: (i, u_col0 // tw + j)),
            pl.BlockSpec((rows, tw), lambda i, j: (i, v_col0 // tw + j)),
            pl.BlockSpec((None, 1, tw), lambda i, j: (layer, 0, j)),
            pl.BlockSpec((None, gpt, CHUNK, CHUNK), lambda i, j: (layer, j, 0, 0)),
            pl.BlockSpec((None, gpt, CHUNK, ch), lambda i, j: (layer, j, 0, 0)),
        ],
        out_specs=pl.BlockSpec((rows, tw), lambda i, j: (i, j)),
        out_shape=jax.ShapeDtypeStruct((m, width), BF16),
        compiler_params=_params("parallel", "parallel"),
        name="spatial_gating",
    )(p, p, gn, w_s, b_s)


def kernel(x_prompt, x_sample, c, cache_ga_k, cache_ga_v, cache_sw_k, cache_sw_v, c_ctx, w_ada, b_ada, g_mix, w_in, qn_a, kn_a, qn_d, kn_d, sink_d, gn_c, w_s, b_s, w_o, g_ffn, w_up, conv_w, conv_b, w_down):
    n_layers = w_ada.shape[0]
    n_ctx, s_ctx, d = x_prompt.shape
    n_lat, s_lat, _ = x_sample.shape
    gw = d // N_MIXERS
    n_heads = gw // HEAD_DIM
    a_kv = cache_ga_k.shape[3]
    d_kv = cache_sw_k.shape[3]
    b_ch = gw // FOURIER_GROUPS
    c_ch = gn_c.shape[2]
    n_past = cache_ga_k.shape[2]
    assert n_lat + 1 <= MOD_ROWS and s_lat % GRID_W == 0

    a_col = 0
    b_col = a_col + gw + 2 * a_kv * HEAD_DIM
    u_col = b_col + gw
    v_col = u_col + gw
    d_col = v_col + gw

    w_in_all, w_o_all = w_in.astype(BF16), w_o.astype(BF16)
    w_in_b = [(w_in_all, l) for l in range(n_layers)]
    w_o_b = [(w_o_all, l) for l in range(n_layers)]
    w_up_b = [w_up[0].astype(BF16)] + [None] * (n_layers - 1)
    w_down_b = [None] * n_layers

    cond = jnp.concatenate([c_ctx[None, :], c], axis=0)
    cond = jnp.pad(cond, ((0, MOD_ROWS - cond.shape[0]), (0, 0)))
    mod = _ada_mod(cond, w_ada, b_ada).reshape(n_layers, MOD_ROWS, N_MOD, 1, d)

    cos, sin_signed = _rope_tables(s_lat)
    dft = {s: _dft_mats(b_ch, s) for s in {s_ctx, s_lat}}
    gn = gn_c.reshape(n_layers, 1, gw)
    b_sb = jnp.broadcast_to(jnp.swapaxes(b_s, 1, 2)[..., None], b_s.shape[:1] + (b_s.shape[2], CHUNK, c_ch))
    cache = [t.reshape(t.shape[0], t.shape[1], n_past, -1)
             for t in (cache_ga_k, cache_ga_v, cache_sw_k, cache_sw_v)]

    ctx_row = lambda i, bm: 0
    lat_row = lambda i, bm: 1 + (i * bm) // s_lat

    def layer(x, l, latent):
        row_of = lat_row if latent else ctx_row
        s_len = s_lat if latent else s_ctx
        h = _norm_mod(x, g_mix, mod, l, 1, 0, row_of)
        p = _in_proj(h, w_in_b[l])
        if latent:
            oa = _attn_lat(p, a_col, a_kv, n_heads // a_kv, s_len, cache[0], cache[1], l, cos,
                           sin_signed, qn_a[l], kn_a[l], None, None)
            od = _attn_lat(p, d_col, d_kv, n_heads // d_kv, s_len, cache[2], cache[3], l, cos,
                           sin_signed, qn_d[l], kn_d[l], sink_d[l], WINDOW)
            new_kv = None
        else:
            oa, ka, va = _attn_ctx(p, a_col, a_kv, n_heads // a_kv, s_len, qn_a[l], kn_a[l], None)
            od, kd, vd = _attn_ctx(p, d_col, d_kv, n_heads // d_kv, s_len, qn_d[l], kn_d[l], sink_d[l])
            new_kv = (ka, va, kd, vd)
        ob = _fourier(p, b_col, gw, b_ch, s_len, *dft[s_len])
        oc = _gating(p, u_col, v_col, gw, c_ch, gn, w_s, b_sb, l)
        x = _out_proj((oa, ob, oc, od), w_o_b[l], x, mod, l, 2, row_of)
        h2 = _norm_mod(x, g_ffn, mod, l, 4, 3, row_of)
        if w_down_b[l] is None:
            side = [(w_down, l)] + ([(w_up, l + 1)] if l + 1 < n_layers else [])
            act, rounded = _ffn_up(h2, w_up_b[l], conv_w, conv_b, l, s_len, side)
            w_down_b[l] = rounded[0]
            if l + 1 < n_layers:
                w_up_b[l + 1] = rounded[1]
        else:
            act, _ = _ffn_up(h2, w_up_b[l], conv_w, conv_b, l, s_len)
        x = _ffn_down(act, w_down_b[l], x, mod, l, 5, row_of)
        return x, new_kv

    xc = x_prompt.reshape(n_ctx * s_ctx, d)
    xl = x_sample.reshape(n_lat * s_lat, d)
    new_kv = []
    for l in range(n_layers):
        xc, kv_l = layer(xc, l, False)
        new_kv.append(kv_l)
    for l in range(n_layers):
        xl, _ = layer(xl, l, True)

    def stack(idx, n_kv):
        return jnp.stack([kv_l[idx].reshape(n_ctx, s_ctx, n_kv, HEAD_DIM) for kv_l in new_kv], axis=1)

    return (xc.reshape(n_ctx, s_ctx, d), xl.reshape(n_lat, s_lat, d),
            stack(0, a_kv), stack(1, a_kv), stack(2, d_kv), stack(3, d_kv))
```

```python
import functools
import math

import jax
import jax.numpy as jnp
import numpy as np
from jax import lax
from jax.experimental import pallas as pl
from jax.experimental.pallas import tpu as pltpu

F32 = jnp.float32
BF16 = jnp.bfloat16

HEAD_DIM = 128
GRID_W = 64
CHUNK = 128
WINDOW = 128
ROPE_THETA = 10000.0
EPS = 1e-6
N_MIXERS = 4
FOURIER_GROUPS = 8
N_MOD = 6
MOD_ROWS = 16

V7X_VMEM_BYTES = 64 * 1024 * 1024
VMEM_LIMIT = V7X_VMEM_BYTES - 8 * 1024 * 1024

TILE_M = 1024
TILE_N = 1024
TILE_N_OUT = 512
TILE_FF = 512
FFN_ROW_PIECES = 8
TILE_K_DOWN = 2560
LANES = 128
BF16_ROWS = 16
NORM_ROWS = 512
ATTN_Q_BLOCK = 256
LOG2E = math.log2(math.e)
MASKED = -1e30


def _params(*sem):
    return pltpu.CompilerParams(dimension_semantics=sem, vmem_limit_bytes=VMEM_LIMIT)


def _dot(a, b):
    return jnp.dot(a, b, preferred_element_type=F32)


def _rms(x, gain):
    ms = jnp.mean(x * x, axis=-1, keepdims=True)
    return (x * lax.rsqrt(ms + EPS)) * gain


def _silu(x):
    return x * (1.0 / (1.0 + jnp.exp(-x)))


def _gelu_tanh(x):
    a = -2.0 * math.sqrt(2.0 / math.pi) * LOG2E
    return x / (1.0 + jnp.exp2(x * (a + (a * 0.044715) * (x * x))))


def _ada_kernel(c_ref, w_ref, b_ref, o_ref):
    a = _silu(c_ref[...]).astype(BF16)
    o_ref[...] = _dot(a, w_ref[...].astype(BF16)) + b_ref[...]


def _ada_mod(cond, w_ada, b_ada):
    n_layers, d, n = w_ada.shape
    tn = TILE_N
    return pl.pallas_call(
        _ada_kernel,
        grid=(n_layers, n // tn),
        in_specs=[
            pl.BlockSpec((MOD_ROWS, d), lambda l, j: (0, 0)),
            pl.BlockSpec((None, d, tn), lambda l, j: (l, 0, j)),
            pl.BlockSpec((None, 1, tn), lambda l, j: (l, 0, j)),
        ],
        out_specs=pl.BlockSpec((None, MOD_ROWS, tn), lambda l, j: (l, 0, j)),
        out_shape=jax.ShapeDtypeStruct((n_layers, MOD_ROWS, n), F32),
        compiler_params=_params("parallel", "parallel"),
        name="ada_mod",
    )(cond, w_ada, b_ada.reshape(n_layers, 1, n))


def _mod_spec(layer, chunk, row_of, width, col_of):
    return pl.BlockSpec(
        (None, None, None, 1, width),
        lambda *g: (layer, row_of(*g), chunk, 0, col_of(*g)))


def _norm_mod_kernel(x_ref, g_ref, sc_ref, sh_ref, o_ref):
    y = _rms(x_ref[...], g_ref[...])
    o_ref[...] = (y * (1.0 + sc_ref[...]) + sh_ref[...]).astype(o_ref.dtype)


def _norm_mod(x, gain, mod, layer, sc_chunk, sh_chunk, row_of_tile):
    m, d = x.shape
    bm = NORM_ROWS
    row_of = lambda i: row_of_tile(i, bm)
    zero = lambda i: 0
    return pl.pallas_call(
        _norm_mod_kernel,
        grid=(m // bm,),
        in_specs=[
            pl.BlockSpec((bm, d), lambda i: (i, 0)),
            pl.BlockSpec((None, 1, d), lambda i: (layer, 0, 0)),
            _mod_spec(layer, sc_chunk, row_of, d, zero),
            _mod_spec(layer, sh_chunk, row_of, d, zero),
        ],
        out_specs=pl.BlockSpec((bm, d), lambda i: (i, 0)),
        out_shape=jax.ShapeDtypeStruct((m, d), BF16),
        compiler_params=_params("parallel"),
        name="norm_mod",
    )(x, gain.reshape(gain.shape[0], 1, d), mod, mod)


def _row_pieces(rows, pieces):
    step = rows // pieces
    return [slice(r, r + step) for r in range(0, rows, step)]


def _proj_kernel(x_ref, w_ref, o_ref):
    o_ref[...] = _dot(x_ref[...], w_ref[...]).astype(o_ref.dtype)


def _weight_spec(w, block, index):
    if isinstance(w, tuple):
        layer = w[1]
        return pl.BlockSpec((None,) + block, lambda *g: (layer,) + index(*g))
    return pl.BlockSpec(block, index)


def _weight_array(w):
    return w[0] if isinstance(w, tuple) else w


def _in_proj(h, w):
    m, k = h.shape
    n = _weight_array(w).shape[-1]
    return pl.pallas_call(
        _proj_kernel,
        grid=(m // TILE_M, n // TILE_N),
        in_specs=[
            pl.BlockSpec((TILE_M, k), lambda i, j: (i, 0)),
            _weight_spec(w, (k, TILE_N), lambda i, j: (0, j)),
        ],
        out_specs=pl.BlockSpec((TILE_M, TILE_N), lambda i, j: (i, j)),
        out_shape=jax.ShapeDtypeStruct((m, n), F32),
        compiler_params=_params("parallel", "parallel"),
        name="in_proj",
    )(h, _weight_array(w))


def _out_proj_kernel(a_ref, b_ref, c_ref, d_ref, w_ref, x_ref, gt_ref, o_ref):
    mixed = jnp.concatenate([a_ref[...], b_ref[...], c_ref[...], d_ref[...]], axis=1)
    o_ref[...] = x_ref[...] + gt_ref[...] * _dot(mixed, w_ref[...])


def _out_proj(mixed, w, x, mod, layer, gt_chunk, row_of_tile):
    m, d = x.shape
    kw = mixed[0].shape[1]
    tn = TILE_N_OUT
    row_of = lambda i, j: row_of_tile(i, TILE_M)
    mix_spec = pl.BlockSpec((TILE_M, kw), lambda i, j: (i, 0))
    return pl.pallas_call(
        _out_proj_kernel,
        grid=(m // TILE_M, d // tn),
        in_specs=[mix_spec] * N_MIXERS + [
            _weight_spec(w, (N_MIXERS * kw, tn), lambda i, j: (0, j)),
            pl.BlockSpec((TILE_M, tn), lambda i, j: (i, j)),
            _mod_spec(layer, gt_chunk, row_of, tn, lambda i, j: j),
        ],
        out_specs=pl.BlockSpec((TILE_M, tn), lambda i, j: (i, j)),
        out_shape=jax.ShapeDtypeStruct((m, d), F32),
        compiler_params=_params("parallel", "parallel"),
        name="out_proj",
    )(*mixed, _weight_array(w), x, mod)


def _ffn_up_kernel(h_ref, wg_ref, wu_ref, cwg_ref, cwu_ref, cbg_ref, cbu_ref, *rest, seq_len):
    n_side = len(rest) // 2
    o_ref = rest[n_side]
    _round_side(rest[:n_side] + rest[n_side + 1:])
    h = h_ref[...]
    rows = h.shape[0]
    pos = lax.rem(lax.broadcasted_iota(jnp.int32, (rows, 1), 0), seq_len)
    first = pos == 0
    last = pos == seq_len - 1

    def conv(z, cw_ref, cb_ref):
        prev = jnp.where(first, 0.0, pltpu.roll(z, 1, 0))
        nxt = jnp.where(last, 0.0, pltpu.roll(z, rows - 1, 0))
        return prev * cw_ref[0:1, :] + z * cw_ref[1:2, :] + nxt * cw_ref[2:3, :] + cb_ref[...]

    def rows_dot(w_ref):
        w = w_ref[...]
        return jnp.concatenate([_dot(h[r], w) for r in _row_pieces(rows, FFN_ROW_PIECES)], axis=0)

    gate = conv(rows_dot(wg_ref), cwg_ref, cbg_ref)
    up = conv(rows_dot(wu_ref), cwu_ref, cbu_ref)
    o_ref[...] = (_silu(gate) * up).astype(o_ref.dtype)


def _side_blocks(r, c, steps):
    options = []
    for cb in range(1, steps + 1):
        if c % cb or (c // cb) % LANES:
            continue
        for rb in range(steps // cb, 0, -1):
            if r % rb == 0 and (r // rb) % BF16_ROWS == 0:
                options.append((rb * cb, -cb, rb))
                break
    _, neg_cb, rb = max(options)
    return rb, -neg_cb


def _side_specs(side, steps, step_of):
    side_in, side_out, side_shape = [], [], []
    for w_src, src_layer in side:
        _, r, c = w_src.shape
        rb, cb = _side_blocks(r, c, steps)

        def at(*g, rb=rb, cb=cb):
            t = jnp.minimum(step_of(*g), rb * cb - 1)
            return (t // cb, t % cb)

        block = (r // rb, c // cb)
        side_in.append(pl.BlockSpec((None,) + block, lambda *g, at=at, sl=src_layer: (sl,) + at(*g)))
        side_out.append(pl.BlockSpec(block, at))
        side_shape.append(jax.ShapeDtypeStruct((r, c), BF16))
    return side_in, side_out, side_shape


def _round_side(side_refs):
    n = len(side_refs) // 2
    for src_ref, dst_ref in zip(side_refs[:n], side_refs[n:]):
        dst_ref[...] = src_ref[...].astype(dst_ref.dtype)


def _ffn_up(h, w_up, conv_w, conv_b, layer, seq_len, side=()):
    m, k = h.shape
    w_up_arr = _weight_array(w_up)
    d_ff = w_up_arr.shape[-1] // 2
    tn = TILE_FF
    nj = d_ff // tn
    ni = m // TILE_M
    assert TILE_M % seq_len == 0
    w_spec = lambda off: _weight_spec(w_up, (k, tn), lambda i, j: (0, off + j))
    cw_spec = lambda off: pl.BlockSpec((None, conv_w.shape[1], tn), lambda i, j: (layer, 0, off + j))
    cb_spec = lambda off: pl.BlockSpec((None, 1, tn), lambda i, j: (layer, 0, off + j))
    conv_b3 = conv_b.reshape(conv_b.shape[0], 1, 2 * d_ff)
    side_in, side_out, side_shape = _side_specs(side, ni * nj, lambda i, j: i * nj + j)
    out = pl.pallas_call(
        functools.partial(_ffn_up_kernel, seq_len=seq_len),
        grid=(ni, nj),
        in_specs=[
            pl.BlockSpec((TILE_M, k), lambda i, j: (i, 0)),
            w_spec(0), w_spec(nj), cw_spec(0), cw_spec(nj), cb_spec(0), cb_spec(nj),
        ] + side_in,
        out_specs=[pl.BlockSpec((TILE_M, tn), lambda i, j: (i, j))] + side_out,
        out_shape=[jax.ShapeDtypeStruct((m, d_ff), BF16)] + side_shape,
        compiler_params=_params("parallel", "parallel"),
        name="ffn_up",
    )(h, w_up_arr, w_up_arr, conv_w, conv_w, conv_b3, conv_b3, *[w_src for w_src, _ in side])
    return out[0], out[1:]


def _ffn_down_kernel(a_ref, w_ref, x_ref, gt_ref, *rest):
    n_side = (len(rest) - 2) // 2
    o_ref, acc_ref = rest[n_side], rest[-1]
    _round_side(rest[:n_side] + rest[n_side + 1:-1])
    k = pl.program_id(2)
    last = pl.num_programs(2) - 1

    @pl.when(k == 0)
    def _():
        acc_ref[...] = _dot(a_ref[...], w_ref[...])

    @pl.when((k > 0) & (k < last))
    def _():
        acc_ref[...] += _dot(a_ref[...], w_ref[...])

    @pl.when(k == last)
    def _():
        o_ref[...] = x_ref[...] + gt_ref[...] * (acc_ref[...] + _dot(a_ref[...], w_ref[...]))


def _ffn_down(act, w, x, mod, layer, gt_chunk, row_of_tile, side=()):
    m, d = x.shape
    d_ff = act.shape[1]
    tk = TILE_K_DOWN
    grid = (m // TILE_M, d // TILE_N, d_ff // tk)
    row_of = lambda i, j, k: row_of_tile(i, TILE_M)
    side_in, side_out, side_shape = _side_specs(
        side, grid[0] * grid[1] * grid[2], lambda i, j, k: (i * grid[1] + j) * grid[2] + k)
    out = pl.pallas_call(
        _ffn_down_kernel,
        grid=grid,
        in_specs=[
            pl.BlockSpec((TILE_M, tk), lambda i, j, k: (i, k)),
            _weight_spec(w, (tk, TILE_N), lambda i, j, k: (k, j)),
            pl.BlockSpec((TILE_M, TILE_N), lambda i, j, k: (i, j)),
            _mod_spec(layer, gt_chunk, row_of, TILE_N, lambda i, j, k: j),
        ] + side_in,
        out_specs=[pl.BlockSpec((TILE_M, TILE_N), lambda i, j, k: (i, j))] + side_out,
        out_shape=[jax.ShapeDtypeStruct((m, d), F32)] + side_shape,
        scratch_shapes=[pltpu.VMEM((TILE_M, TILE_N), F32)],
        compiler_params=_params("parallel", "parallel", "arbitrary"),
        name="ffn_down",
    )(act, _weight_array(w), x, mod, *[w_src for w_src, _ in side])
    return out[0], out[1:]


def _softmax_pv(q, kt, v_ones, bias, sink):
    d = q.shape[1]
    scale = d ** -0.5
    s = _dot(q, kt)
    if bias is not None:
        s = s + bias
    m = jnp.max(s, axis=-1, keepdims=True) * scale
    if sink is not None:
        m = jnp.maximum(m, sink)
    m2 = m * LOG2E
    e = jnp.exp2(s * (scale * LOG2E) - m2).astype(BF16)
    ov = _dot(e, v_ones)
    den = ov[:, d:]
    if sink is not None:
        den = den + jnp.exp2(sink * LOG2E - m2)
    return ov[:, :d] / den


def _with_ones(v):
    return jnp.concatenate([v, jnp.ones_like(v)], axis=1)


def _attn_ctx_kernel(*refs, n_kv, groups, has_sink):
    if has_sink:
        qkv_ref, qn_ref, kn_ref, sink_ref, o_ref, ko_ref, vo_ref = refs
    else:
        qkv_ref, qn_ref, kn_ref, o_ref, ko_ref, vo_ref = refs
    qn = qn_ref[...]
    kn = kn_ref[...]
    k0 = n_kv * groups * HEAD_DIM
    v0 = k0 + n_kv * HEAD_DIM
    for h in range(n_kv):
        k = _rms(qkv_ref[:, k0 + h * HEAD_DIM:k0 + (h + 1) * HEAD_DIM], kn)
        v = qkv_ref[:, v0 + h * HEAD_DIM:v0 + (h + 1) * HEAD_DIM]
        ko_ref[:, h, :] = k
        vo_ref[:, h, :] = v
        kt = k.T.astype(BF16)
        v_ones = _with_ones(v.astype(BF16))
        for g in range(groups):
            head = h * groups + g
            cols = slice(head * HEAD_DIM, (head + 1) * HEAD_DIM)
            q = _rms(qkv_ref[:, cols], qn).astype(BF16)
            sink = sink_ref[head] if has_sink else None
            o_ref[:, cols] = _softmax_pv(q, kt, v_ones, None, sink).astype(o_ref.dtype)


def _attn_ctx(p, col0, n_kv, groups, s_len, qn, kn, sink):
    m = p.shape[0]
    qw = n_kv * groups * HEAD_DIM
    width = qw + 2 * n_kv * HEAD_DIM
    assert col0 % width == 0
    has_sink = sink is not None
    vec = pl.BlockSpec((1, HEAD_DIM), lambda b: (0, 0))
    in_specs = [pl.BlockSpec((s_len, width), lambda b: (b, col0 // width)), vec, vec]
    args = [p, qn.reshape(1, HEAD_DIM), kn.reshape(1, HEAD_DIM)]
    if has_sink:
        in_specs.append(pl.BlockSpec(memory_space=pltpu.SMEM))
        args.append(sink)
    kv_spec = pl.BlockSpec((None, s_len, n_kv, HEAD_DIM), lambda b: (b, 0, 0, 0))
    kv_shape = jax.ShapeDtypeStruct((m // s_len, s_len, n_kv, HEAD_DIM), F32)
    return pl.pallas_call(
        functools.partial(_attn_ctx_kernel, n_kv=n_kv, groups=groups, has_sink=has_sink),
        grid=(m // s_len,),
        in_specs=in_specs,
        out_specs=[pl.BlockSpec((s_len, qw), lambda b: (b, 0)), kv_spec, kv_spec],
        out_shape=[jax.ShapeDtypeStruct((m, qw), BF16), kv_shape, kv_shape],
        compiler_params=_params("parallel"),
        name="attn_ctx",
    )(*args)


def _rope(x, cos, sin_signed):
    lane = lax.broadcasted_iota(jnp.int32, x.shape, 1)
    partner = jnp.where((lane & (HEAD_DIM // 4)) == 0,
                        pltpu.roll(x, HEAD_DIM - HEAD_DIM // 4, 1),
                        pltpu.roll(x, HEAD_DIM // 4, 1))
    return x * cos + partner * sin_signed


def _attn_lat_kernel(*refs, groups, has_sink, window, n_past, q_block):
    if has_sink:
        (q_ref, k_ref, v_ref, kc_ref, vc_ref, cos_ref, sin_ref, qn_ref, kn_ref, sink_ref,
         o_ref) = refs
    else:
        q_ref, k_ref, v_ref, kc_ref, vc_ref, cos_ref, sin_ref, qn_ref, kn_ref, o_ref = refs
    s_len = q_ref.shape[0]
    kv = pl.program_id(1)
    qn = qn_ref[...]
    k = _rope(_rms(k_ref[...], kn_ref[...]), cos_ref[...], sin_ref[...])
    kt = jnp.concatenate([kc_ref[...].T, k.T], axis=1).astype(BF16)
    v_ones = _with_ones(jnp.concatenate([vc_ref[...], v_ref[...]], axis=0).astype(BF16))
    biases = {}
    for q0 in range(0, s_len, q_block):
        rows = slice(q0, q0 + q_block)
        if window is None:
            kt_blk, v_blk, bias = kt, v_ones, None
        else:
            span = q_block + 2 * window
            start = min(max(q0 - window, 0), s_len - span)
            band = slice(n_past + start, n_past + start + span)
            kt_blk = jnp.concatenate([kt[:, :n_past], kt[:, band]], axis=1)
            v_blk = jnp.concatenate([v_ones[:n_past], v_ones[band]], axis=0)
            if q0 - start not in biases:
                col = lax.broadcasted_iota(jnp.int32, (q_block, n_past + span), 1)
                row = lax.broadcasted_iota(jnp.int32, (q_block, n_past + span), 0)
                dist = jnp.abs((row + (q0 - start)) - (col - n_past))
                biases[q0 - start] = jnp.where((col < n_past) | (dist <= window), 0.0, MASKED)
            bias = biases[q0 - start]
        cos = cos_ref[rows, :]
        sin = sin_ref[rows, :]
        for g in range(groups):
            cols = slice(g * HEAD_DIM, (g + 1) * HEAD_DIM)
            q = _rope(_rms(q_ref[rows, cols], qn), cos, sin).astype(BF16)
            sink = sink_ref[kv * groups + g] if has_sink else None
            o_ref[rows, cols] = _softmax_pv(q, kt_blk, v_blk, bias, sink).astype(o_ref.dtype)


def _attn_lat(p, col0, n_kv, groups, s_len, cache_k, cache_v, layer, cos, sin_signed,
              qn, kn, sink, window):
    m = p.shape[0]
    qw = groups * HEAD_DIM
    n_past = cache_k.shape[2]
    k0 = (col0 + n_kv * qw) // HEAD_DIM
    v0 = k0 + n_kv
    has_sink = sink is not None
    vec = pl.BlockSpec((1, HEAD_DIM), lambda b, h: (0, 0))
    cache_spec = pl.BlockSpec((None, None, n_past, HEAD_DIM), lambda b, h: (b, layer, 0, h))
    table = pl.BlockSpec((s_len, HEAD_DIM), lambda b, h: (0, 0))
    in_specs = [
        pl.BlockSpec((s_len, qw), lambda b, h: (b, col0 // qw + h)),
        pl.BlockSpec((s_len, HEAD_DIM), lambda b, h: (b, k0 + h)),
        pl.BlockSpec((s_len, HEAD_DIM), lambda b, h: (b, v0 + h)),
        cache_spec, cache_spec, table, table, vec, vec,
    ]
    args = [p, p, p, cache_k, cache_v, cos, sin_signed,
            qn.reshape(1, HEAD_DIM), kn.reshape(1, HEAD_DIM)]
    if has_sink:
        in_specs.append(pl.BlockSpec(memory_space=pltpu.SMEM))
        args.append(sink)
    return pl.pallas_call(
        functools.partial(_attn_lat_kernel, groups=groups, has_sink=has_sink, window=window,
                          n_past=n_past, q_block=ATTN_Q_BLOCK),
        grid=(m // s_len, n_kv),
        in_specs=in_specs,
        out_specs=pl.BlockSpec((s_len, qw), lambda b, h: (b, h)),
        out_shape=jax.ShapeDtypeStruct((m, n_kv * qw), BF16),
        compiler_params=_params("parallel", "parallel"),
        name="attn_lat",
    )(*args)


def _rope_tables(s_len):
    quarter = HEAD_DIM // 4
    inv = ROPE_THETA ** (-jnp.arange(quarter, dtype=F32) / quarter)
    t = jnp.arange(s_len)
    ang_row = (t // GRID_W).astype(F32)[:, None] * inv[None, :]
    ang_col = (t % GRID_W).astype(F32)[:, None] * inv[None, :]
    cos = jnp.concatenate([jnp.cos(ang_row)] * 2 + [jnp.cos(ang_col)] * 2, axis=-1)
    sin = jnp.concatenate([-jnp.sin(ang_row), jnp.sin(ang_row),
                           -jnp.sin(ang_col), jnp.sin(ang_col)], axis=-1)
    return cos, sin


def _fourier_kernel(x_ref, cs_ref, f_ref, o_ref, *, ch, scale):
    x = x_ref[...].astype(BF16)
    cs = cs_ref[...]
    parts = [_dot(x[:, g * ch:(g + 1) * ch], cs) for g in range(x.shape[1] // ch)]
    y = jnp.concatenate(
        [jnp.concatenate([t[:, :ch] for t in parts], axis=1),
         jnp.concatenate([t[:, ch:] for t in parts], axis=1)], axis=0).astype(BF16)
    o_ref[...] = (_dot(f_ref[...], y) * scale).astype(o_ref.dtype)


def _dft_cos_sin(n):
    j = np.arange(n, dtype=np.int64)
    ang = ((j[:, None] * j[None, :]) % n).astype(np.float64) * (2.0 * math.pi / n)
    return np.cos(ang).astype(np.float32), np.sin(ang).astype(np.float32)


def _dft_mats(ch, s_len):
    cc, sc = _dft_cos_sin(ch)
    cp, sp = _dft_cos_sin(s_len)
    return (jnp.asarray(np.concatenate([cc, sc], axis=1), BF16),
            jnp.asarray(np.concatenate([cp, -sp], axis=1), BF16))


def _fourier(p, col0, width, ch, s_len, cs_mat, f_mat):
    m = p.shape[0]
    tw = 512
    return pl.pallas_call(
        functools.partial(_fourier_kernel, ch=ch, scale=1.0 / math.sqrt(s_len * ch)),
        grid=(m // s_len, width // tw),
        in_specs=[
            pl.BlockSpec((s_len, tw), lambda b, j: (b, col0 // tw + j)),
            pl.BlockSpec((ch, 2 * ch), lambda b, j: (0, 0)),
            pl.BlockSpec((s_len, 2 * s_len), lambda b, j: (0, 0)),
        ],
        out_specs=pl.BlockSpec((s_len, tw), lambda b, j: (b, j)),
        out_shape=jax.ShapeDtypeStruct((m, width), BF16),
        compiler_params=_params("parallel", "parallel"),
        name="fourier",
    )(p, cs_mat, f_mat)


def _gating_kernel(u_ref, v_ref, gn_ref, w_ref, b_ref, o_ref, *, ch):
    rows = u_ref.shape[0]
    n_chunks = rows // CHUNK
    u = _gelu_tanh(u_ref[...])
    v = _gelu_tanh(v_ref[...])
    for g in range(u.shape[1] // ch):
        lanes = slice(g * ch, (g + 1) * ch)
        vn = _rms(v[:, lanes], gn_ref[:, lanes]).astype(BF16)
        rhs = jnp.concatenate([vn[c * CHUNK:(c + 1) * CHUNK] for c in range(n_chunks)], axis=1)
        sv = _dot(w_ref[g].astype(BF16), rhs)
        bias = b_ref[g]
        for c in range(n_chunks):
            rs = slice(c * CHUNK, (c + 1) * CHUNK)
            o_ref[rs, lanes] = (u[rs, lanes] * (sv[:, c * ch:(c + 1) * ch] + bias)).astype(o_ref.dtype)


def _gating(p, u_col0, v_col0, width, ch, gn, w_s, b_s, layer):
    m = p.shape[0]
    tw = 512
    rows = 512
    gpt = tw // ch
    return pl.pallas_call(
        functools.partial(_gating_kernel, ch=ch),
        grid=(m // rows, width // tw),
        in_specs=[
            pl.BlockSpec((rows, tw), lambda i, j: (i, u_col0 // tw + j)),
            pl.BlockSpec((rows, tw), lambda i, j: (i, v_col0 // tw + j)),
            pl.BlockSpec((None, 1, tw), lambda i, j: (layer, 0, j)),
            pl.BlockSpec((None, gpt, CHUNK, CHUNK), lambda i, j: (layer, j, 0, 0)),
            pl.BlockSpec((None, gpt, CHUNK, ch), lambda i, j: (layer, j, 0, 0)),
        ],
        out_specs=pl.BlockSpec((rows, tw), lambda i, j: (i, j)),
        out_shape=jax.ShapeDtypeStruct((m, width), BF16),
        compiler_params=_params("parallel", "parallel"),
        name="spatial_gating",
    )(p, p, gn, w_s, b_s)


def kernel(x_prompt, x_sample, c, cache_ga_k, cache_ga_v, cache_sw_k, cache_sw_v, c_ctx, w_ada, b_ada, g_mix, w_in, qn_a, kn_a, qn_d, kn_d, sink_d, gn_c, w_s, b_s, w_o, g_ffn, w_up, conv_w, conv_b, w_down):
    n_layers = w_ada.shape[0]
    n_ctx, s_ctx, d = x_prompt.shape
    n_lat, s_lat, _ = x_sample.shape
    gw = d // N_MIXERS
    n_heads = gw // HEAD_DIM
    a_kv = cache_ga_k.shape[3]
    d_kv = cache_sw_k.shape[3]
    b_ch = gw // FOURIER_GROUPS
    c_ch = gn_c.shape[2]
    n_past = cache_ga_k.shape[2]
    assert n_lat + 1 <= MOD_ROWS and s_lat % GRID_W == 0

    a_col = 0
    b_col = a_col + gw + 2 * a_kv * HEAD_DIM
    u_col = b_col + gw
    v_col = u_col + gw
    d_col = v_col + gw

    w_in_b = [w_in[0].astype(BF16)] + [None] * (n_layers - 1)
    w_o_b = [w_o[0].astype(BF16)] + [None] * (n_layers - 1)
    w_up_b = [w_up[0].astype(BF16)] + [None] * (n_layers - 1)
    w_down_b = [None] * n_layers

    cond = jnp.concatenate([c_ctx[None, :], c], axis=0)
    cond = jnp.pad(cond, ((0, MOD_ROWS - cond.shape[0]), (0, 0)))
    mod = _ada_mod(cond, w_ada, b_ada).reshape(n_layers, MOD_ROWS, N_MOD, 1, d)

    cos, sin_signed = _rope_tables(s_lat)
    dft = {s: _dft_mats(b_ch, s) for s in {s_ctx, s_lat}}
    gn = gn_c.reshape(n_layers, 1, gw)
    b_sb = jnp.broadcast_to(jnp.swapaxes(b_s, 1, 2)[..., None], b_s.shape[:1] + (b_s.shape[2], CHUNK, c_ch))
    cache = [t.reshape(t.shape[0], t.shape[1], n_past, -1)
             for t in (cache_ga_k, cache_ga_v, cache_sw_k, cache_sw_v)]

    ctx_row = lambda i, bm: 0
    lat_row = lambda i, bm: 1 + (i * bm) // s_lat

    def layer(x, l, latent):
        row_of = lat_row if latent else ctx_row
        s_len = s_lat if latent else s_ctx
        h = _norm_mod(x, g_mix, mod, l, 1, 0, row_of)
        p = _in_proj(h, w_in_b[l])
        if latent:
            oa = _attn_lat(p, a_col, a_kv, n_heads // a_kv, s_len, cache[0], cache[1], l, cos,
                           sin_signed, qn_a[l], kn_a[l], None, None)
            od = _attn_lat(p, d_col, d_kv, n_heads // d_kv, s_len, cache[2], cache[3], l, cos,
                           sin_signed, qn_d[l], kn_d[l], sink_d[l], WINDOW)
            new_kv = None
        else:
            oa, ka, va = _attn_ctx(p, a_col, a_kv, n_heads // a_kv, s_len, qn_a[l], kn_a[l], None)
            od, kd, vd = _attn_ctx(p, d_col, d_kv, n_heads // d_kv, s_len, qn_d[l], kn_d[l], sink_d[l])
            new_kv = (ka, va, kd, vd)
        ob = _fourier(p, b_col, gw, b_ch, s_len, *dft[s_len])
        oc = _gating(p, u_col, v_col, gw, c_ch, gn, w_s, b_sb, l)
        x = _out_proj((oa, ob, oc, od), w_o_b[l], x, mod, l, 2, row_of)
        h2 = _norm_mod(x, g_ffn, mod, l, 4, 3, row_of)
        if w_down_b[l] is None:
            more = l + 1 < n_layers
            side = [(w_down, l)] + ([(w_up, l + 1)] if more else [])
            act, rounded = _ffn_up(h2, w_up_b[l], conv_w, conv_b, l, s_len, side)
            w_down_b[l] = rounded[0]
            side = [(w_in, l + 1), (w_o, l + 1)] if more else []
            x, rounded_next = _ffn_down(act, w_down_b[l], x, mod, l, 5, row_of, side)
            if more:
                w_up_b[l + 1] = rounded[1]
                w_in_b[l + 1], w_o_b[l + 1] = rounded_next
        else:
            act, _ = _ffn_up(h2, w_up_b[l], conv_w, conv_b, l, s_len)
            x, _ = _ffn_down(act, w_down_b[l], x, mod, l, 5, row_of)
        return x, new_kv

    xc = x_prompt.reshape(n_ctx * s_ctx, d)
    xl = x_sample.reshape(n_lat * s_lat, d)
    new_kv = []
    for l in range(n_layers):
        xc, kv_l = layer(xc, l, False)
        new_kv.append(kv_l)
    for l in range(n_layers):
        xl, _ = layer(xl, l, True)

    def stack(idx):
        return jnp.stack([kv_l[idx] for kv_l in new_kv], axis=1)

    return (xc.reshape(n_ctx, s_ctx, d), xl.reshape(n_lat, s_lat, d),
            stack(0), stack(1), stack(2), stack(3))
```

```python
import functools
import math

import jax
import jax.numpy as jnp
import numpy as np
from jax import lax
from jax.experimental import pallas as pl
from jax.experimental.pallas import tpu as pltpu

F32 = jnp.float32
BF16 = jnp.bfloat16

HEAD_DIM = 128
GRID_W = 64
CHUNK = 128
WINDOW = 128
ROPE_THETA = 10000.0
EPS = 1e-6
N_MIXERS = 4
FOURIER_GROUPS = 8
N_MOD = 6
MOD_ROWS = 16

V7X_VMEM_BYTES = 64 * 1024 * 1024
VMEM_LIMIT = V7X_VMEM_BYTES - 8 * 1024 * 1024

TILE_M = 1024
TILE_N = 1024
TILE_N_OUT = 512
TILE_FF = 512
FFN_ROW_PIECES = 8
TILE_K_DOWN = 2560
LANES = 128
BF16_ROWS = 16
MIXER_ROWS = 1024
MIXER_COLS = 512
NORM_ROWS = 512
NORM_COLS = 512
NORM_PIECE = 128
ATTN_Q_BLOCK = 256
LOG2E = math.log2(math.e)
MASKED = -1e30


def _params(*sem):
    return pltpu.CompilerParams(dimension_semantics=sem, vmem_limit_bytes=VMEM_LIMIT)


def _dot(a, b):
    return jnp.dot(a, b, preferred_element_type=F32)


def _rms(x, gain):
    ms = jnp.mean(x * x, axis=-1, keepdims=True)
    return (x * lax.rsqrt(ms + EPS)) * gain


def _silu(x):
    return x * (1.0 / (1.0 + jnp.exp(-x)))


def _gelu_tanh(x):
    a = -2.0 * math.sqrt(2.0 / math.pi) * LOG2E
    return x / (1.0 + jnp.exp2(x * (a + (a * 0.044715) * (x * x))))


def _ada_kernel(c_ref, w_ref, b_ref, o_ref):
    a = _silu(c_ref[...]).astype(BF16)
    o_ref[...] = _dot(a, w_ref[...].astype(BF16)) + b_ref[...]


def _ada_mod(cond, w_ada, b_ada):
    n_layers, d, n = w_ada.shape
    tn = TILE_N
    return pl.pallas_call(
        _ada_kernel,
        grid=(n_layers, n // tn),
        in_specs=[
            pl.BlockSpec((MOD_ROWS, d), lambda l, j: (0, 0)),
            pl.BlockSpec((None, d, tn), lambda l, j: (l, 0, j)),
            pl.BlockSpec((None, 1, tn), lambda l, j: (l, 0, j)),
        ],
        out_specs=pl.BlockSpec((None, MOD_ROWS, tn), lambda l, j: (l, 0, j)),
        out_shape=jax.ShapeDtypeStruct((n_layers, MOD_ROWS, n), F32),
        compiler_params=_params("parallel", "parallel"),
        name="ada_mod",
    )(cond, w_ada, b_ada.reshape(n_layers, 1, n))


def _mod_spec(layer, chunk, row_of, width, col_of):
    return pl.BlockSpec(
        (None, None, None, 1, width),
        lambda *g: (layer, row_of(*g), chunk, 0, col_of(*g)))


def _norm_mod_kernel(x_ref, g_ref, sc_ref, sh_ref, o_ref):
    rows, d = x_ref.shape
    for r0 in range(0, rows, NORM_PIECE):
        rs = slice(r0, r0 + NORM_PIECE)
        acc = jnp.zeros((NORM_PIECE, LANES), F32)
        for c0 in range(0, d, LANES):
            xc = x_ref[rs, c0:c0 + LANES]
            acc = acc + xc * xc
        inv = lax.rsqrt(jnp.sum(acc, axis=-1, keepdims=True) * (1.0 / d) + EPS)
        for c0 in range(0, d, NORM_COLS):
            cols = slice(c0, c0 + NORM_COLS)
            y = (x_ref[rs, cols] * inv) * g_ref[:, cols]
            o_ref[rs, cols] = (y * (1.0 + sc_ref[:, cols]) + sh_ref[:, cols]).astype(o_ref.dtype)


def _norm_mod(x, gain, mod, layer, sc_chunk, sh_chunk, row_of_tile):
    m, d = x.shape
    bm = NORM_ROWS
    row_of = lambda i: row_of_tile(i, bm)
    zero = lambda i: 0
    return pl.pallas_call(
        _norm_mod_kernel,
        grid=(m // bm,),
        in_specs=[
            pl.BlockSpec((bm, d), lambda i: (i, 0)),
            pl.BlockSpec((None, 1, d), lambda i: (layer, 0, 0)),
            _mod_spec(layer, sc_chunk, row_of, d, zero),
            _mod_spec(layer, sh_chunk, row_of, d, zero),
        ],
        out_specs=pl.BlockSpec((bm, d), lambda i: (i, 0)),
        out_shape=jax.ShapeDtypeStruct((m, d), BF16),
        compiler_params=_params("parallel"),
        name="norm_mod",
    )(x, gain.reshape(gain.shape[0], 1, d), mod, mod)


def _row_pieces(rows, pieces):
    step = rows // pieces
    return [slice(r, r + step) for r in range(0, rows, step)]


def _proj_kernel(x_ref, w_ref, o_ref):
    o_ref[...] = _dot(x_ref[...], w_ref[...]).astype(o_ref.dtype)


def _weight_spec(w, block, index):
    if isinstance(w, tuple):
        layer = w[1]
        return pl.BlockSpec((None,) + block, lambda *g: (layer,) + index(*g))
    return pl.BlockSpec(block, index)


def _weight_array(w):
    return w[0] if isinstance(w, tuple) else w


def _in_proj(h, w):
    m, k = h.shape
    n = _weight_array(w).shape[-1]
    return pl.pallas_call(
        _proj_kernel,
        grid=(m // TILE_M, n // TILE_N),
        in_specs=[
            pl.BlockSpec((TILE_M, k), lambda i, j: (i, 0)),
            _weight_spec(w, (k, TILE_N), lambda i, j: (0, j)),
        ],
        out_specs=pl.BlockSpec((TILE_M, TILE_N), lambda i, j: (i, j)),
        out_shape=jax.ShapeDtypeStruct((m, n), F32),
        compiler_params=_params("parallel", "parallel"),
        name="in_proj",
    )(h, _weight_array(w))


def _out_proj_kernel(a_ref, b_ref, c_ref, d_ref, w_ref, x_ref, gt_ref, o_ref):
    mixed = jnp.concatenate([a_ref[...], b_ref[...], c_ref[...], d_ref[...]], axis=1)
    o_ref[...] = x_ref[...] + gt_ref[...] * _dot(mixed, w_ref[...])


def _out_proj(mixed, w, x, mod, layer, gt_chunk, row_of_tile):
    m, d = x.shape
    kw = mixed[0].shape[1]
    tn = TILE_N_OUT
    row_of = lambda i, j: row_of_tile(i, TILE_M)
    mix_spec = pl.BlockSpec((TILE_M, kw), lambda i, j: (i, 0))
    return pl.pallas_call(
        _out_proj_kernel,
        grid=(m // TILE_M, d // tn),
        in_specs=[mix_spec] * N_MIXERS + [
            _weight_spec(w, (N_MIXERS * kw, tn), lambda i, j: (0, j)),
            pl.BlockSpec((TILE_M, tn), lambda i, j: (i, j)),
            _mod_spec(layer, gt_chunk, row_of, tn, lambda i, j: j),
        ],
        out_specs=pl.BlockSpec((TILE_M, tn), lambda i, j: (i, j)),
        out_shape=jax.ShapeDtypeStruct((m, d), F32),
        compiler_params=_params("parallel", "parallel"),
        name="out_proj",
    )(*mixed, _weight_array(w), x, mod)


def _ffn_up_kernel(h_ref, wg_ref, wu_ref, cwg_ref, cwu_ref, cbg_ref, cbu_ref, *rest, seq_len):
    n_side = len(rest) // 2
    o_ref = rest[n_side]
    _round_side(rest[:n_side] + rest[n_side + 1:])
    h = h_ref[...]
    rows = h.shape[0]
    pos = lax.rem(lax.broadcasted_iota(jnp.int32, (rows, 1), 0), seq_len)
    first = pos == 0
    last = pos == seq_len - 1

    def conv(z, cw_ref, cb_ref):
        prev = jnp.where(first, 0.0, pltpu.roll(z, 1, 0))
        nxt = jnp.where(last, 0.0, pltpu.roll(z, rows - 1, 0))
        return prev * cw_ref[0:1, :] + z * cw_ref[1:2, :] + nxt * cw_ref[2:3, :] + cb_ref[...]

    def rows_dot(w_ref):
        w = w_ref[...]
        return jnp.concatenate([_dot(h[r], w) for r in _row_pieces(rows, FFN_ROW_PIECES)], axis=0)

    gate = conv(rows_dot(wg_ref), cwg_ref, cbg_ref)
    up = conv(rows_dot(wu_ref), cwu_ref, cbu_ref)
    o_ref[...] = (_silu(gate) * up).astype(o_ref.dtype)


def _side_blocks(r, c, steps):
    options = []
    for cb in range(1, steps + 1):
        if c % cb or (c // cb) % LANES:
            continue
        for rb in range(steps // cb, 0, -1):
            if r % rb == 0 and (r // rb) % BF16_ROWS == 0:
                options.append((rb * cb, -cb, rb))
                break
    _, neg_cb, rb = max(options)
    return rb, -neg_cb


def _side_specs(side, steps, step_of):
    side_in, side_out, side_shape = [], [], []
    for w_src, src_layer in side:
        _, r, c = w_src.shape
        rb, cb = _side_blocks(r, c, steps)

        def at(*g, rb=rb, cb=cb):
            t = jnp.minimum(step_of(*g), rb * cb - 1)
            return (t // cb, t % cb)

        block = (r // rb, c // cb)
        side_in.append(pl.BlockSpec((None,) + block, lambda *g, at=at, sl=src_layer: (sl,) + at(*g)))
        side_out.append(pl.BlockSpec(block, at))
        side_shape.append(jax.ShapeDtypeStruct((r, c), BF16))
    return side_in, side_out, side_shape


def _round_side(side_refs):
    n = len(side_refs) // 2
    for src_ref, dst_ref in zip(side_refs[:n], side_refs[n:]):
        dst_ref[...] = src_ref[...].astype(dst_ref.dtype)


def _ffn_up(h, w_up, conv_w, conv_b, layer, seq_len, side=()):
    m, k = h.shape
    w_up_arr = _weight_array(w_up)
    d_ff = w_up_arr.shape[-1] // 2
    tn = TILE_FF
    nj = d_ff // tn
    ni = m // TILE_M
    assert TILE_M % seq_len == 0
    w_spec = lambda off: _weight_spec(w_up, (k, tn), lambda i, j: (0, off + j))
    cw_spec = lambda off: pl.BlockSpec((None, conv_w.shape[1], tn), lambda i, j: (layer, 0, off + j))
    cb_spec = lambda off: pl.BlockSpec((None, 1, tn), lambda i, j: (layer, 0, off + j))
    conv_b3 = conv_b.reshape(conv_b.shape[0], 1, 2 * d_ff)
    side_in, side_out, side_shape = _side_specs(side, ni * nj, lambda i, j: i * nj + j)
    out = pl.pallas_call(
        functools.partial(_ffn_up_kernel, seq_len=seq_len),
        grid=(ni, nj),
        in_specs=[
            pl.BlockSpec((TILE_M, k), lambda i, j: (i, 0)),
            w_spec(0), w_spec(nj), cw_spec(0), cw_spec(nj), cb_spec(0), cb_spec(nj),
        ] + side_in,
        out_specs=[pl.BlockSpec((TILE_M, tn), lambda i, j: (i, j))] + side_out,
        out_shape=[jax.ShapeDtypeStruct((m, d_ff), BF16)] + side_shape,
        compiler_params=_params("parallel", "parallel"),
        name="ffn_up",
    )(h, w_up_arr, w_up_arr, conv_w, conv_w, conv_b3, conv_b3, *[w_src for w_src, _ in side])
    return out[0], out[1:]


def _ffn_down_kernel(a_ref, w_ref, x_ref, gt_ref, *rest):
    n_side = (len(rest) - 2) // 2
    o_ref, acc_ref = rest[n_side], rest[-1]
    _round_side(rest[:n_side] + rest[n_side + 1:-1])
    k = pl.program_id(2)
    last = pl.num_programs(2) - 1

    @pl.when(k == 0)
    def _():
        acc_ref[...] = _dot(a_ref[...], w_ref[...])

    @pl.when((k > 0) & (k < last))
    def _():
        acc_ref[...] += _dot(a_ref[...], w_ref[...])

    @pl.when(k == last)
    def _():
        o_ref[...] = x_ref[...] + gt_ref[...] * (acc_ref[...] + _dot(a_ref[...], w_ref[...]))


def _ffn_down(act, w, x, mod, layer, gt_chunk, row_of_tile, side=()):
    m, d = x.shape
    d_ff = act.shape[1]
    tk = TILE_K_DOWN
    grid = (m // TILE_M, d // TILE_N, d_ff // tk)
    row_of = lambda i, j, k: row_of_tile(i, TILE_M)
    side_in, side_out, side_shape = _side_specs(
        side, grid[0] * grid[1] * grid[2], lambda i, j, k: (i * grid[1] + j) * grid[2] + k)
    out = pl.pallas_call(
        _ffn_down_kernel,
        grid=grid,
        in_specs=[
            pl.BlockSpec((TILE_M, tk), lambda i, j, k: (i, k)),
            _weight_spec(w, (tk, TILE_N), lambda i, j, k: (k, j)),
            pl.BlockSpec((TILE_M, TILE_N), lambda i, j, k: (i, j)),
            _mod_spec(layer, gt_chunk, row_of, TILE_N, lambda i, j, k: j),
        ] + side_in,
        out_specs=[pl.BlockSpec((TILE_M, TILE_N), lambda i, j, k: (i, j))] + side_out,
        out_shape=[jax.ShapeDtypeStruct((m, d), F32)] + side_shape,
        scratch_shapes=[pltpu.VMEM((TILE_M, TILE_N), F32)],
        compiler_params=_params("parallel", "parallel", "arbitrary"),
        name="ffn_down",
    )(act, _weight_array(w), x, mod, *[w_src for w_src, _ in side])
    return out[0], out[1:]


def _softmax_pv(q, kt, v_ones, bias, sink):
    d = q.shape[1]
    scale = d ** -0.5
    s = _dot(q, kt)
    if bias is not None:
        s = s + bias
    m = jnp.max(s, axis=-1, keepdims=True) * scale
    if sink is not None:
        m = jnp.maximum(m, sink)
    m2 = m * LOG2E
    e = jnp.exp2(s * (scale * LOG2E) - m2).astype(BF16)
    ov = _dot(e, v_ones)
    den = ov[:, d:]
    if sink is not None:
        den = den + jnp.exp2(sink * LOG2E - m2)
    return ov[:, :d] / den


def _with_ones(v):
    return jnp.concatenate([v, jnp.ones_like(v)], axis=1)


def _attn_ctx_kernel(*refs, n_kv, groups, has_sink):
    qkv_ref, qn_ref, kn_ref = refs[:3]
    sink_ref = refs[3] if has_sink else None
    o_ref, ko_ref, vo_ref = refs[-3:]
    qn = qn_ref[...]
    kn = kn_ref[...]
    k0 = n_kv * groups * HEAD_DIM
    v0 = k0 + n_kv * HEAD_DIM
    for h in range(n_kv):
        k = _rms(qkv_ref[:, k0 + h * HEAD_DIM:k0 + (h + 1) * HEAD_DIM], kn)
        v = qkv_ref[:, v0 + h * HEAD_DIM:v0 + (h + 1) * HEAD_DIM]
        ko_ref[:, h, :] = k
        vo_ref[:, h, :] = v
        kt = k.T.astype(BF16)
        v_ones = _with_ones(v.astype(BF16))
        for g in range(groups):
            head = h * groups + g
            cols = slice(head * HEAD_DIM, (head + 1) * HEAD_DIM)
            q = _rms(qkv_ref[:, cols], qn).astype(BF16)
            sink = sink_ref[head] if has_sink else None
            o_ref[:, cols] = _softmax_pv(q, kt, v_ones, None, sink).astype(o_ref.dtype)


def _attn_ctx(p, col0, n_kv, groups, s_len, qn, kn, sink, layer, n_layers, kv_prev):
    m = p.shape[0]
    qw = n_kv * groups * HEAD_DIM
    width = qw + 2 * n_kv * HEAD_DIM
    assert col0 % width == 0
    has_sink = sink is not None
    vec = pl.BlockSpec((1, HEAD_DIM), lambda b: (0, 0))
    in_specs = [pl.BlockSpec((s_len, width), lambda b: (b, col0 // width)), vec, vec]
    args = [p, qn.reshape(1, HEAD_DIM), kn.reshape(1, HEAD_DIM)]
    if has_sink:
        in_specs.append(pl.BlockSpec(memory_space=pltpu.SMEM))
        args.append(sink)
    kv_spec = pl.BlockSpec((None, None, s_len, n_kv, HEAD_DIM), lambda b: (b, layer, 0, 0, 0))
    kv_shape = jax.ShapeDtypeStruct((m // s_len, n_layers, s_len, n_kv, HEAD_DIM), F32)
    aliases = {}
    if kv_prev is not None:
        aliases = {len(args): 1, len(args) + 1: 2}
        in_specs += [pl.BlockSpec(memory_space=pl.ANY)] * 2
        args += list(kv_prev)
    return pl.pallas_call(
        functools.partial(_attn_ctx_kernel, n_kv=n_kv, groups=groups, has_sink=has_sink),
        grid=(m // s_len,),
        in_specs=in_specs,
        out_specs=[pl.BlockSpec((s_len, qw), lambda b: (b, 0)), kv_spec, kv_spec],
        out_shape=[jax.ShapeDtypeStruct((m, qw), BF16), kv_shape, kv_shape],
        input_output_aliases=aliases,
        compiler_params=_params("parallel"),
        name="attn_ctx",
    )(*args)


def _rope(x, cos, sin_signed):
    lane = lax.broadcasted_iota(jnp.int32, x.shape, 1)
    partner = jnp.where((lane & (HEAD_DIM // 4)) == 0,
                        pltpu.roll(x, HEAD_DIM - HEAD_DIM // 4, 1),
                        pltpu.roll(x, HEAD_DIM // 4, 1))
    return x * cos + partner * sin_signed


def _attn_lat_kernel(*refs, groups, has_sink, window, n_past, q_block):
    if has_sink:
        (q_ref, k_ref, v_ref, kc_ref, vc_ref, cos_ref, sin_ref, qn_ref, kn_ref, sink_ref,
         o_ref) = refs
    else:
        q_ref, k_ref, v_ref, kc_ref, vc_ref, cos_ref, sin_ref, qn_ref, kn_ref, o_ref = refs
    s_len = q_ref.shape[0]
    kv = pl.program_id(1)
    qn = qn_ref[...]
    k = _rope(_rms(k_ref[...], kn_ref[...]), cos_ref[...], sin_ref[...])
    kt = jnp.concatenate([kc_ref[...].T, k.T], axis=1).astype(BF16)
    v_ones = _with_ones(jnp.concatenate([vc_ref[...], v_ref[...]], axis=0).astype(BF16))
    biases = {}
    for q0 in range(0, s_len, q_block):
        rows = slice(q0, q0 + q_block)
        if window is None:
            kt_blk, v_blk, bias = kt, v_ones, None
        else:
            span = q_block + 2 * window
            start = min(max(q0 - window, 0), s_len - span)
            band = slice(n_past + start, n_past + start + span)
            kt_blk = jnp.concatenate([kt[:, :n_past], kt[:, band]], axis=1)
            v_blk = jnp.concatenate([v_ones[:n_past], v_ones[band]], axis=0)
            if q0 - start not in biases:
                col = lax.broadcasted_iota(jnp.int32, (q_block, n_past + span), 1)
                row = lax.broadcasted_iota(jnp.int32, (q_block, n_past + span), 0)
                dist = jnp.abs((row + (q0 - start)) - (col - n_past))
                biases[q0 - start] = jnp.where((col < n_past) | (dist <= window), 0.0, MASKED)
            bias = biases[q0 - start]
        cos = cos_ref[rows, :]
        sin = sin_ref[rows, :]
        for g in range(groups):
            cols = slice(g * HEAD_DIM, (g + 1) * HEAD_DIM)
            q = _rope(_rms(q_ref[rows, cols], qn), cos, sin).astype(BF16)
            sink = sink_ref[kv * groups + g] if has_sink else None
            o_ref[rows, cols] = _softmax_pv(q, kt_blk, v_blk, bias, sink).astype(o_ref.dtype)


def _attn_lat(p, col0, n_kv, groups, s_len, cache_k, cache_v, layer, cos, sin_signed,
              qn, kn, sink, window):
    m = p.shape[0]
    qw = groups * HEAD_DIM
    n_past = cache_k.shape[2]
    k0 = (col0 + n_kv * qw) // HEAD_DIM
    v0 = k0 + n_kv
    has_sink = sink is not None
    vec = pl.BlockSpec((1, HEAD_DIM), lambda b, h: (0, 0))
    cache_spec = pl.BlockSpec((None, None, n_past, HEAD_DIM), lambda b, h: (b, layer, 0, h))
    table = pl.BlockSpec((s_len, HEAD_DIM), lambda b, h: (0, 0))
    in_specs = [
        pl.BlockSpec((s_len, qw), lambda b, h: (b, col0 // qw + h)),
        pl.BlockSpec((s_len, HEAD_DIM), lambda b, h: (b, k0 + h)),
        pl.BlockSpec((s_len, HEAD_DIM), lambda b, h: (b, v0 + h)),
        cache_spec, cache_spec, table, table, vec, vec,
    ]
    args = [p, p, p, cache_k, cache_v, cos, sin_signed,
            qn.reshape(1, HEAD_DIM), kn.reshape(1, HEAD_DIM)]
    if has_sink:
        in_specs.append(pl.BlockSpec(memory_space=pltpu.SMEM))
        args.append(sink)
    return pl.pallas_call(
        functools.partial(_attn_lat_kernel, groups=groups, has_sink=has_sink, window=window,
                          n_past=n_past, q_block=ATTN_Q_BLOCK),
        grid=(m // s_len, n_kv),
        in_specs=in_specs,
        out_specs=pl.BlockSpec((s_len, qw), lambda b, h: (b, h)),
        out_shape=jax.ShapeDtypeStruct((m, n_kv * qw), BF16),
        compiler_params=_params("parallel", "parallel"),
        name="attn_lat",
    )(*args)


def _rope_tables(s_len):
    quarter = HEAD_DIM // 4
    inv = ROPE_THETA ** (-jnp.arange(quarter, dtype=F32) / quarter)
    t = jnp.arange(s_len)
    ang_row = (t // GRID_W).astype(F32)[:, None] * inv[None, :]
    ang_col = (t % GRID_W).astype(F32)[:, None] * inv[None, :]
    cos = jnp.concatenate([jnp.cos(ang_row)] * 2 + [jnp.cos(ang_col)] * 2, axis=-1)
    sin = jnp.concatenate([-jnp.sin(ang_row), jnp.sin(ang_row),
                           -jnp.sin(ang_col), jnp.sin(ang_col)], axis=-1)
    return cos, sin


def _fourier_kernel(x_ref, cs_ref, f_ref, o_ref, *, ch, scale):
    s_len = f_ref.shape[0]
    cs = cs_ref[...]
    for s0 in range(0, x_ref.shape[0], s_len):
        x = x_ref[s0:s0 + s_len, :].astype(BF16)
        parts = [_dot(x[:, g * ch:(g + 1) * ch], cs) for g in range(x.shape[1] // ch)]
        y = jnp.concatenate(
            [jnp.concatenate([t[:, :ch] for t in parts], axis=1),
             jnp.concatenate([t[:, ch:] for t in parts], axis=1)], axis=0).astype(BF16)
        o_ref[s0:s0 + s_len, :] = (_dot(f_ref[...], y) * scale).astype(o_ref.dtype)


def _dft_cos_sin(n):
    j = np.arange(n, dtype=np.int64)
    ang = ((j[:, None] * j[None, :]) % n).astype(np.float64) * (2.0 * math.pi / n)
    return np.cos(ang).astype(np.float32), np.sin(ang).astype(np.float32)


def _dft_mats(ch, s_len):
    cc, sc = _dft_cos_sin(ch)
    cp, sp = _dft_cos_sin(s_len)
    return (jnp.asarray(np.concatenate([cc, sc], axis=1), BF16),
            jnp.asarray(np.concatenate([cp, -sp], axis=1), BF16))


def _fourier(p, col0, width, ch, s_len, cs_mat, f_mat):
    m = p.shape[0]
    tw = MIXER_COLS
    rows = max(s_len, MIXER_ROWS)
    return pl.pallas_call(
        functools.partial(_fourier_kernel, ch=ch, scale=1.0 / math.sqrt(s_len * ch)),
        grid=(m // rows, width // tw),
        in_specs=[
            pl.BlockSpec((rows, tw), lambda b, j: (b, col0 // tw + j)),
            pl.BlockSpec((ch, 2 * ch), lambda b, j: (0, 0)),
            pl.BlockSpec((s_len, 2 * s_len), lambda b, j: (0, 0)),
        ],
        out_specs=pl.BlockSpec((rows, tw), lambda b, j: (b, j)),
        out_shape=jax.ShapeDtypeStruct((m, width), BF16),
        compiler_params=_params("parallel", "parallel"),
        name="fourier",
    )(p, cs_mat, f_mat)


def _gating_kernel(u_ref, v_ref, gn_ref, w_ref, b_ref, o_ref, *, ch):
    rows = u_ref.shape[0]
    n_chunks = rows // CHUNK
    u = _gelu_tanh(u_ref[...])
    v = _gelu_tanh(v_ref[...])
    for g in range(u.shape[1] // ch):
        lanes = slice(g * ch, (g + 1) * ch)
        vn = _rms(v[:, lanes], gn_ref[:, lanes]).astype(BF16)
        rhs = jnp.concatenate([vn[c * CHUNK:(c + 1) * CHUNK] for c in range(n_chunks)], axis=1)
        sv = _dot(w_ref[g].astype(BF16), rhs)
        bias = b_ref[g]
        for c in range(n_chunks):
            rs = slice(c * CHUNK, (c + 1) * CHUNK)
            o_ref[rs, lanes] = (u[rs, lanes] * (sv[:, c * ch:(c + 1) * ch] + bias)).astype(o_ref.dtype)


def _gating(p, u_col0, v_col0, width, ch, gn, w_s, b_s, layer):
    m = p.shape[0]
    tw = MIXER_COLS
    rows = MIXER_ROWS
    gpt = tw // ch
    return pl.pallas_call(
        functools.partial(_gating_kernel, ch=ch),
        grid=(m // rows, width // tw),
        in_specs=[
            pl.BlockSpec((rows, tw), lambda i, j: (i, u_col0 // tw + j)),
            pl.BlockSpec((rows, tw), lambda i, j: (i, v_col0 // tw + j)),
            pl.BlockSpec((None, 1, tw), lambda i, j: (layer, 0, j)),
            pl.BlockSpec((None, gpt, CHUNK, CHUNK), lambda i, j: (layer, j, 0, 0)),
            pl.BlockSpec((None, gpt, CHUNK, ch), lambda i, j: (layer, j, 0, 0)),
        ],
        out_specs=pl.BlockSpec((rows, tw), lambda i, j: (i, j)),
        out_shape=jax.ShapeDtypeStruct((m, width), BF16),
        compiler_params=_params("parallel", "parallel"),
        name="spatial_gating",
    )(p, p, gn, w_s, b_s)


def kernel(x_prompt, x_sample, c, cache_ga_k, cache_ga_v, cache_sw_k, cache_sw_v, c_ctx, w_ada, b_ada, g_mix, w_in, qn_a, kn_a, qn_d, kn_d, sink_d, gn_c, w_s, b_s, w_o, g_ffn, w_up, conv_w, conv_b, w_down):
    n_layers = w_ada.shape[0]
    n_ctx, s_ctx, d = x_prompt.shape
    n_lat, s_lat, _ = x_sample.shape
    gw = d // N_MIXERS
    n_heads = gw // HEAD_DIM
    a_kv = cache_ga_k.shape[3]
    d_kv = cache_sw_k.shape[3]
    b_ch = gw // FOURIER_GROUPS
    c_ch = gn_c.shape[2]
    n_past = cache_ga_k.shape[2]
    assert n_lat + 1 <= MOD_ROWS and s_lat % GRID_W == 0

    a_col = 0
    b_col = a_col + gw + 2 * a_kv * HEAD_DIM
    u_col = b_col + gw
    v_col = u_col + gw
    d_col = v_col + gw

    w_in_b = [w_in[0].astype(BF16)] + [None] * (n_layers - 1)
    w_o_b = [w_o[0].astype(BF16)] + [None] * (n_layers - 1)
    w_up_b = [w_up[0].astype(BF16)] + [None] * (n_layers - 1)
    w_down_b = [None] * n_layers

    cond = jnp.concatenate([c_ctx[None, :], c], axis=0)
    cond = jnp.pad(cond, ((0, MOD_ROWS - cond.shape[0]), (0, 0)))
    mod = _ada_mod(cond, w_ada, b_ada).reshape(n_layers, MOD_ROWS, N_MOD, 1, d)

    cos, sin_signed = _rope_tables(s_lat)
    dft = {s: _dft_mats(b_ch, s) for s in {s_ctx, s_lat}}
    gn = gn_c.reshape(n_layers, 1, gw)
    b_sb = jnp.broadcast_to(jnp.swapaxes(b_s, 1, 2)[..., None], b_s.shape[:1] + (b_s.shape[2], CHUNK, c_ch))
    cache = [t.reshape(t.shape[0], t.shape[1], n_past, -1)
             for t in (cache_ga_k, cache_ga_v, cache_sw_k, cache_sw_v)]

    ctx_row = lambda i, bm: 0
    lat_row = lambda i, bm: 1 + (i * bm) // s_lat

    def layer(x, l, latent, new_kv=None):
        row_of = lat_row if latent else ctx_row
        s_len = s_lat if latent else s_ctx
        h = _norm_mod(x, g_mix, mod, l, 1, 0, row_of)
        p = _in_proj(h, w_in_b[l])
        if latent:
            oa = _attn_lat(p, a_col, a_kv, n_heads // a_kv, s_len, cache[0], cache[1], l, cos,
                           sin_signed, qn_a[l], kn_a[l], None, None)
            od = _attn_lat(p, d_col, d_kv, n_heads // d_kv, s_len, cache[2], cache[3], l, cos,
                           sin_signed, qn_d[l], kn_d[l], sink_d[l], WINDOW)
        else:
            prev_a, prev_d = (None, None) if new_kv is None else (new_kv[:2], new_kv[2:])
            oa, ka, va = _attn_ctx(p, a_col, a_kv, n_heads // a_kv, s_len, qn_a[l], kn_a[l], None,
                                   l, n_layers, prev_a)
            od, kd, vd = _attn_ctx(p, d_col, d_kv, n_heads // d_kv, s_len, qn_d[l], kn_d[l],
                                   sink_d[l], l, n_layers, prev_d)
            new_kv = (ka, va, kd, vd)
        ob = _fourier(p, b_col, gw, b_ch, s_len, *dft[s_len])
        oc = _gating(p, u_col, v_col, gw, c_ch, gn, w_s, b_sb, l)
        x = _out_proj((oa, ob, oc, od), w_o_b[l], x, mod, l, 2, row_of)
        h2 = _norm_mod(x, g_ffn, mod, l, 4, 3, row_of)
        if w_down_b[l] is None:
            more = l + 1 < n_layers
            side = [(w_down, l)] + ([(w_up, l + 1)] if more else [])
            act, rounded = _ffn_up(h2, w_up_b[l], conv_w, conv_b, l, s_len, side)
            w_down_b[l] = rounded[0]
            side = [(w_in, l + 1), (w_o, l + 1)] if more else []
            x, rounded_next = _ffn_down(act, w_down_b[l], x, mod, l, 5, row_of, side)
            if more:
                w_up_b[l + 1] = rounded[1]
                w_in_b[l + 1], w_o_b[l + 1] = rounded_next
        else:
            act, _ = _ffn_up(h2, w_up_b[l], conv_w, conv_b, l, s_len)
            x, _ = _ffn_down(act, w_down_b[l], x, mod, l, 5, row_of)
        return x, new_kv

    xc = x_prompt.reshape(n_ctx * s_ctx, d)
    xl = x_sample.reshape(n_lat * s_lat, d)
    new_kv = None
    for l in range(n_layers):
        xc, new_kv = layer(xc, l, False, new_kv)
    for l in range(n_layers):
        xl, _ = layer(xl, l, True)

    return (xc.reshape(n_ctx, s_ctx, d), xl.reshape(n_lat, s_lat, d)) + tuple(new_kv)
```

```python
import functools
import math

import jax
import jax.numpy as jnp
import numpy as np
from jax import lax
from jax.experimental import pallas as pl
from jax.experimental.pallas import tpu as pltpu

F32 = jnp.float32
BF16 = jnp.bfloat16

HEAD_DIM = 128
GRID_W = 64
CHUNK = 128
WINDOW = 128
ROPE_THETA = 10000.0
EPS = 1e-6
N_MIXERS = 4
FOURIER_GROUPS = 8
N_MOD = 6
MOD_ROWS = 16

V7X_VMEM_BYTES = 64 * 1024 * 1024
VMEM_LIMIT = V7X_VMEM_BYTES - 8 * 1024 * 1024

TILE_M = 1024
TILE_N = 1024
TILE_N_OUT = 512
TILE_FF = 512
FFN_ROW_PIECES = 8
TILE_K_DOWN = 2560
LANES = 128
BF16_ROWS = 16
MIXER_ROWS = 1024
MIXER_COLS = 512
NORM_ROWS = 1024
NORM_COLS = 512
NORM_PIECE = 128
ATTN_Q_BLOCK = 256
LOG2E = math.log2(math.e)
MASKED = -1e30


def _params(*sem):
    return pltpu.CompilerParams(dimension_semantics=sem, vmem_limit_bytes=VMEM_LIMIT)


def _dot(a, b):
    return jnp.dot(a, b, preferred_element_type=F32)


def _rms(x, gain):
    ms = jnp.mean(x * x, axis=-1, keepdims=True)
    return (x * lax.rsqrt(ms + EPS)) * gain


def _silu(x):
    return x * (1.0 / (1.0 + jnp.exp(-x)))


def _gelu_tanh(x):
    a = -2.0 * math.sqrt(2.0 / math.pi) * LOG2E
    return x / (1.0 + jnp.exp2(x * (a + (a * 0.044715) * (x * x))))


def _ada_kernel(c_ref, w_ref, b_ref, o_ref):
    a = _silu(c_ref[...]).astype(BF16)
    o_ref[...] = _dot(a, w_ref[...].astype(BF16)) + b_ref[...]


def _ada_mod(cond, w_ada, b_ada):
    n_layers, d, n = w_ada.shape
    tn = TILE_N
    return pl.pallas_call(
        _ada_kernel,
        grid=(n_layers, n // tn),
        in_specs=[
            pl.BlockSpec((MOD_ROWS, d), lambda l, j: (0, 0)),
            pl.BlockSpec((None, d, tn), lambda l, j: (l, 0, j)),
            pl.BlockSpec((None, 1, tn), lambda l, j: (l, 0, j)),
        ],
        out_specs=pl.BlockSpec((None, MOD_ROWS, tn), lambda l, j: (l, 0, j)),
        out_shape=jax.ShapeDtypeStruct((n_layers, MOD_ROWS, n), F32),
        compiler_params=_params("parallel", "parallel"),
        name="ada_mod",
    )(cond, w_ada, b_ada.reshape(n_layers, 1, n))


def _mod_spec(layer, chunk, row_of, width, col_of):
    return pl.BlockSpec(
        (None, None, None, 1, width),
        lambda *g: (layer, row_of(*g), chunk, 0, col_of(*g)))


def _norm_mod_kernel(x_ref, g_ref, sc_ref, sh_ref, o_ref):
    rows, d = x_ref.shape
    for r0 in range(0, rows, NORM_PIECE):
        rs = slice(r0, r0 + NORM_PIECE)
        acc = jnp.zeros((NORM_PIECE, LANES), F32)
        for c0 in range(0, d, LANES):
            xc = x_ref[rs, c0:c0 + LANES]
            acc = acc + xc * xc
        inv = lax.rsqrt(jnp.sum(acc, axis=-1, keepdims=True) * (1.0 / d) + EPS)
        for c0 in range(0, d, NORM_COLS):
            cols = slice(c0, c0 + NORM_COLS)
            y = (x_ref[rs, cols] * inv) * g_ref[:, cols]
            o_ref[rs, cols] = (y * (1.0 + sc_ref[:, cols]) + sh_ref[:, cols]).astype(o_ref.dtype)


def _norm_mod(x, gain, mod, layer, sc_chunk, sh_chunk, row_of_tile):
    m, d = x.shape
    bm = NORM_ROWS
    row_of = lambda i: row_of_tile(i, bm)
    zero = lambda i: 0
    return pl.pallas_call(
        _norm_mod_kernel,
        grid=(m // bm,),
        in_specs=[
            pl.BlockSpec((bm, d), lambda i: (i, 0)),
            pl.BlockSpec((None, 1, d), lambda i: (layer, 0, 0)),
            _mod_spec(layer, sc_chunk, row_of, d, zero),
            _mod_spec(layer, sh_chunk, row_of, d, zero),
        ],
        out_specs=pl.BlockSpec((bm, d), lambda i: (i, 0)),
        out_shape=jax.ShapeDtypeStruct((m, d), BF16),
        compiler_params=_params("parallel"),
        name="norm_mod",
    )(x, gain.reshape(gain.shape[0], 1, d), mod, mod)


def _row_pieces(rows, pieces):
    step = rows // pieces
    return [slice(r, r + step) for r in range(0, rows, step)]


def _proj_kernel(x_ref, w_ref, *rest):
    n_side = len(rest) // 2
    o_ref = rest[n_side]
    _round_side(rest[:n_side] + rest[n_side + 1:])
    o_ref[...] = _dot(x_ref[...], w_ref[...]).astype(o_ref.dtype)


def _weight_spec(w, block, index):
    if isinstance(w, tuple):
        layer = w[1]
        return pl.BlockSpec((None,) + block, lambda *g: (layer,) + index(*g))
    return pl.BlockSpec(block, index)


def _weight_array(w):
    return w[0] if isinstance(w, tuple) else w


def _in_proj(h, w, side=()):
    m, k = h.shape
    n = _weight_array(w).shape[-1]
    tn = TILE_N // 2 if side else TILE_N
    nj = n // tn
    side_in, side_out, side_shape = _side_specs(side, (m // TILE_M) * nj, lambda i, j: i * nj + j)
    out = pl.pallas_call(
        _proj_kernel,
        grid=(m // TILE_M, nj),
        in_specs=[
            pl.BlockSpec((TILE_M, k), lambda i, j: (i, 0)),
            _weight_spec(w, (k, tn), lambda i, j: (0, j)),
        ] + side_in,
        out_specs=[pl.BlockSpec((TILE_M, tn), lambda i, j: (i, j))] + side_out,
        out_shape=[jax.ShapeDtypeStruct((m, n), F32)] + side_shape,
        compiler_params=_params(*(("arbitrary",) * 2 if side else ("parallel",) * 2)),
        name="in_proj",
    )(h, _weight_array(w), *[w_src for w_src, _ in side])
    return out[0], out[1:]


def _out_proj_kernel(a_ref, b_ref, c_ref, d_ref, w_ref, x_ref, gt_ref, o_ref):
    mixed = jnp.concatenate([a_ref[...], b_ref[...], c_ref[...], d_ref[...]], axis=1)
    o_ref[...] = x_ref[...] + gt_ref[...] * _dot(mixed, w_ref[...])


def _out_proj(mixed, w, x, mod, layer, gt_chunk, row_of_tile):
    m, d = x.shape
    kw = mixed[0].shape[1]
    tn = TILE_N_OUT
    row_of = lambda i, j: row_of_tile(i, TILE_M)
    mix_spec = pl.BlockSpec((TILE_M, kw), lambda i, j: (i, 0))
    return pl.pallas_call(
        _out_proj_kernel,
        grid=(m // TILE_M, d // tn),
        in_specs=[mix_spec] * N_MIXERS + [
            _weight_spec(w, (N_MIXERS * kw, tn), lambda i, j: (0, j)),
            pl.BlockSpec((TILE_M, tn), lambda i, j: (i, j)),
            _mod_spec(layer, gt_chunk, row_of, tn, lambda i, j: j),
        ],
        out_specs=pl.BlockSpec((TILE_M, tn), lambda i, j: (i, j)),
        out_shape=jax.ShapeDtypeStruct((m, d), F32),
        compiler_params=_params("parallel", "parallel"),
        name="out_proj",
    )(*mixed, _weight_array(w), x, mod)


def _ffn_up_kernel(h_ref, wg_ref, wu_ref, cwg_ref, cwu_ref, cbg_ref, cbu_ref, *rest, seq_len):
    n_side = len(rest) // 2
    o_ref = rest[n_side]
    _round_side(rest[:n_side] + rest[n_side + 1:])
    h = h_ref[...]
    rows = h.shape[0]
    pos = lax.rem(lax.broadcasted_iota(jnp.int32, (rows, 1), 0), seq_len)
    first = pos == 0
    last = pos == seq_len - 1

    def conv(z, cw_ref, cb_ref):
        prev = jnp.where(first, 0.0, pltpu.roll(z, 1, 0))
        nxt = jnp.where(last, 0.0, pltpu.roll(z, rows - 1, 0))
        return prev * cw_ref[0:1, :] + z * cw_ref[1:2, :] + nxt * cw_ref[2:3, :] + cb_ref[...]

    def rows_dot(w_ref):
        w = w_ref[...]
        return jnp.concatenate([_dot(h[r], w) for r in _row_pieces(rows, FFN_ROW_PIECES)], axis=0)

    gate = conv(rows_dot(wg_ref), cwg_ref, cbg_ref)
    up = conv(rows_dot(wu_ref), cwu_ref, cbu_ref)
    o_ref[...] = (_silu(gate) * up).astype(o_ref.dtype)


def _side_blocks(r, c, steps):
    options = []
    for cb in range(1, steps + 1):
        if c % cb or (c // cb) % LANES:
            continue
        for rb in range(steps // cb, 0, -1):
            if r % rb == 0 and (r // rb) % BF16_ROWS == 0:
                options.append((rb * cb, -cb, rb))
                break
    _, neg_cb, rb = max(options)
    return rb, -neg_cb


def _side_specs(side, steps, step_of):
    side_in, side_out, side_shape = [], [], []
    for w_src, src_layer in side:
        _, r, c = w_src.shape
        rb, cb = _side_blocks(r, c, steps)

        def at(*g, rb=rb, cb=cb):
            t = jnp.minimum(step_of(*g), rb * cb - 1)
            return (t // cb, t % cb)

        block = (r // rb, c // cb)
        side_in.append(pl.BlockSpec((None,) + block, lambda *g, at=at, sl=src_layer: (sl,) + at(*g)))
        side_out.append(pl.BlockSpec(block, at))
        side_shape.append(jax.ShapeDtypeStruct((r, c), BF16))
    return side_in, side_out, side_shape


def _round_side(side_refs):
    n = len(side_refs) // 2
    for src_ref, dst_ref in zip(side_refs[:n], side_refs[n:]):
        dst_ref[...] = src_ref[...].astype(dst_ref.dtype)


def _ffn_up(h, w_up, conv_w, conv_b, layer, seq_len, side=()):
    m, k = h.shape
    w_up_arr = _weight_array(w_up)
    d_ff = w_up_arr.shape[-1] // 2
    tn = TILE_FF
    nj = d_ff // tn
    ni = m // TILE_M
    assert TILE_M % seq_len == 0
    w_spec = lambda off: _weight_spec(w_up, (k, tn), lambda i, j: (0, off + j))
    cw_spec = lambda off: pl.BlockSpec((None, conv_w.shape[1], tn), lambda i, j: (layer, 0, off + j))
    cb_spec = lambda off: pl.BlockSpec((None, 1, tn), lambda i, j: (layer, 0, off + j))
    conv_b3 = conv_b.reshape(conv_b.shape[0], 1, 2 * d_ff)
    side_in, side_out, side_shape = _side_specs(side, ni * nj, lambda i, j: i * nj + j)
    out = pl.pallas_call(
        functools.partial(_ffn_up_kernel, seq_len=seq_len),
        grid=(ni, nj),
        in_specs=[
            pl.BlockSpec((TILE_M, k), lambda i, j: (i, 0)),
            w_spec(0), w_spec(nj), cw_spec(0), cw_spec(nj), cb_spec(0), cb_spec(nj),
        ] + side_in,
        out_specs=[pl.BlockSpec((TILE_M, tn), lambda i, j: (i, j))] + side_out,
        out_shape=[jax.ShapeDtypeStruct((m, d_ff), BF16)] + side_shape,
        compiler_params=_params(*(("arbitrary",) * 2 if side else ("parallel",) * 2)),
        name="ffn_up",
    )(h, w_up_arr, w_up_arr, conv_w, conv_w, conv_b3, conv_b3, *[w_src for w_src, _ in side])
    return out[0], out[1:]


def _ffn_down_kernel(a_ref, w_ref, x_ref, gt_ref, *rest):
    n_side = (len(rest) - 2) // 2
    o_ref, acc_ref = rest[n_side], rest[-1]
    _round_side(rest[:n_side] + rest[n_side + 1:-1])
    k = pl.program_id(2)
    last = pl.num_programs(2) - 1

    @pl.when(k == 0)
    def _():
        acc_ref[...] = _dot(a_ref[...], w_ref[...])

    @pl.when((k > 0) & (k < last))
    def _():
        acc_ref[...] += _dot(a_ref[...], w_ref[...])

    @pl.when(k == last)
    def _():
        o_ref[...] = x_ref[...] + gt_ref[...] * (acc_ref[...] + _dot(a_ref[...], w_ref[...]))


def _ffn_down(act, w, x, mod, layer, gt_chunk, row_of_tile, side=()):
    m, d = x.shape
    d_ff = act.shape[1]
    tk = TILE_K_DOWN
    grid = (m // TILE_M, d // TILE_N, d_ff // tk)
    row_of = lambda i, j, k: row_of_tile(i, TILE_M)
    side_in, side_out, side_shape = _side_specs(
        side, grid[0] * grid[1] * grid[2], lambda i, j, k: (i * grid[1] + j) * grid[2] + k)
    out = pl.pallas_call(
        _ffn_down_kernel,
        grid=grid,
        in_specs=[
            pl.BlockSpec((TILE_M, tk), lambda i, j, k: (i, k)),
            _weight_spec(w, (tk, TILE_N), lambda i, j, k: (k, j)),
            pl.BlockSpec((TILE_M, TILE_N), lambda i, j, k: (i, j)),
            _mod_spec(layer, gt_chunk, row_of, TILE_N, lambda i, j, k: j),
        ] + side_in,
        out_specs=[pl.BlockSpec((TILE_M, TILE_N), lambda i, j, k: (i, j))] + side_out,
        out_shape=[jax.ShapeDtypeStruct((m, d), F32)] + side_shape,
        scratch_shapes=[pltpu.VMEM((TILE_M, TILE_N), F32)],
        compiler_params=_params(*(("arbitrary",) * 3 if side else ("parallel", "parallel", "arbitrary"))),
        name="ffn_down",
    )(act, _weight_array(w), x, mod, *[w_src for w_src, _ in side])
    return out[0], out[1:]


def _softmax_pv(q, kt, v_ones, bias, sink):
    d = q.shape[1]
    scale = d ** -0.5
    s = _dot(q, kt)
    if bias is not None:
        s = s + bias
    m = jnp.max(s, axis=-1, keepdims=True) * scale
    if sink is not None:
        m = jnp.maximum(m, sink)
    m2 = m * LOG2E
    e = jnp.exp2(s * (scale * LOG2E) - m2).astype(BF16)
    ov = _dot(e, v_ones)
    den = ov[:, d:]
    if sink is not None:
        den = den + jnp.exp2(sink * LOG2E - m2)
    return ov[:, :d] / den


def _with_ones(v):
    return jnp.concatenate([v, jnp.ones_like(v)], axis=1)


def _attn_ctx_kernel(*refs, n_kv, groups, has_sink):
    qkv_ref, qn_ref, kn_ref = refs[:3]
    sink_ref = refs[3] if has_sink else None
    o_ref, ko_ref, vo_ref = refs[-3:]
    qn = qn_ref[...]
    kn = kn_ref[...]
    k0 = n_kv * groups * HEAD_DIM
    v0 = k0 + n_kv * HEAD_DIM
    for h in range(n_kv):
        k = _rms(qkv_ref[:, k0 + h * HEAD_DIM:k0 + (h + 1) * HEAD_DIM], kn)
        v = qkv_ref[:, v0 + h * HEAD_DIM:v0 + (h + 1) * HEAD_DIM]
        ko_ref[:, h, :] = k
        vo_ref[:, h, :] = v
        kt = k.T.astype(BF16)
        v_ones = _with_ones(v.astype(BF16))
        for g in range(groups):
            head = h * groups + g
            cols = slice(head * HEAD_DIM, (head + 1) * HEAD_DIM)
            q = _rms(qkv_ref[:, cols], qn).astype(BF16)
            sink = sink_ref[head] if has_sink else None
            o_ref[:, cols] = _softmax_pv(q, kt, v_ones, None, sink).astype(o_ref.dtype)


def _attn_ctx(p, col0, n_kv, groups, s_len, qn, kn, sink, layer, n_layers, kv_prev):
    m = p.shape[0]
    qw = n_kv * groups * HEAD_DIM
    width = qw + 2 * n_kv * HEAD_DIM
    assert col0 % width == 0
    has_sink = sink is not None
    vec = pl.BlockSpec((1, HEAD_DIM), lambda b: (0, 0))
    in_specs = [pl.BlockSpec((s_len, width), lambda b: (b, col0 // width)), vec, vec]
    args = [p, qn.reshape(1, HEAD_DIM), kn.reshape(1, HEAD_DIM)]
    if has_sink:
        in_specs.append(pl.BlockSpec(memory_space=pltpu.SMEM))
        args.append(sink)
    kv_spec = pl.BlockSpec((None, None, s_len, n_kv, HEAD_DIM), lambda b: (b, layer, 0, 0, 0))
    kv_shape = jax.ShapeDtypeStruct((m // s_len, n_layers, s_len, n_kv, HEAD_DIM), F32)
    aliases = {}
    if kv_prev is not None:
        aliases = {len(args): 1, len(args) + 1: 2}
        in_specs += [pl.BlockSpec(memory_space=pl.ANY)] * 2
        args += list(kv_prev)
    return pl.pallas_call(
        functools.partial(_attn_ctx_kernel, n_kv=n_kv, groups=groups, has_sink=has_sink),
        grid=(m // s_len,),
        in_specs=in_specs,
        out_specs=[pl.BlockSpec((s_len, qw), lambda b: (b, 0)), kv_spec, kv_spec],
        out_shape=[jax.ShapeDtypeStruct((m, qw), BF16), kv_shape, kv_shape],
        input_output_aliases=aliases,
        compiler_params=_params("parallel"),
        name="attn_ctx",
    )(*args)


def _rope(x, cos, sin_signed):
    lane = lax.broadcasted_iota(jnp.int32, x.shape, 1)
    partner = jnp.where((lane & (HEAD_DIM // 4)) == 0,
                        pltpu.roll(x, HEAD_DIM - HEAD_DIM // 4, 1),
                        pltpu.roll(x, HEAD_DIM // 4, 1))
    return x * cos + partner * sin_signed


def _attn_lat_kernel(*refs, groups, has_sink, window, n_past, q_block):
    if has_sink:
        (q_ref, k_ref, v_ref, kc_ref, vc_ref, cos_ref, sin_ref, qn_ref, kn_ref, sink_ref,
         o_ref) = refs
    else:
        q_ref, k_ref, v_ref, kc_ref, vc_ref, cos_ref, sin_ref, qn_ref, kn_ref, o_ref = refs
    s_len = q_ref.shape[0]
    kv = pl.program_id(1)
    qn = qn_ref[...]
    k = _rope(_rms(k_ref[...], kn_ref[...]), cos_ref[...], sin_ref[...])
    kt = jnp.concatenate([kc_ref[...].T, k.T], axis=1).astype(BF16)
    v_ones = _with_ones(jnp.concatenate([vc_ref[...], v_ref[...]], axis=0).astype(BF16))
    biases = {}
    for q0 in range(0, s_len, q_block):
        rows = slice(q0, q0 + q_block)
        if window is None:
            kt_blk, v_blk, bias = kt, v_ones, None
        else:
            span = q_block + 2 * window
            start = min(max(q0 - window, 0), s_len - span)
            band = slice(n_past + start, n_past + start + span)
            kt_blk = jnp.concatenate([kt[:, :n_past], kt[:, band]], axis=1)
            v_blk = jnp.concatenate([v_ones[:n_past], v_ones[band]], axis=0)
            if q0 - start not in biases:
                col = lax.broadcasted_iota(jnp.int32, (q_block, n_past + span), 1)
                row = lax.broadcasted_iota(jnp.int32, (q_block, n_past + span), 0)
                dist = jnp.abs((row + (q0 - start)) - (col - n_past))
                biases[q0 - start] = jnp.where((col < n_past) | (dist <= window), 0.0, MASKED)
            bias = biases[q0 - start]
        cos = cos_ref[rows, :]
        sin = sin_ref[rows, :]
        for g in range(groups):
            cols = slice(g * HEAD_DIM, (g + 1) * HEAD_DIM)
            q = _rope(_rms(q_ref[rows, cols], qn), cos, sin).astype(BF16)
            sink = sink_ref[kv * groups + g] if has_sink else None
            o_ref[rows, cols] = _softmax_pv(q, kt_blk, v_blk, bias, sink).astype(o_ref.dtype)


def _attn_lat(p, col0, n_kv, groups, s_len, cache_k, cache_v, layer, cos, sin_signed,
              qn, kn, sink, window):
    m = p.shape[0]
    qw = groups * HEAD_DIM
    n_past = cache_k.shape[2]
    k0 = (col0 + n_kv * qw) // HEAD_DIM
    v0 = k0 + n_kv
    has_sink = sink is not None
    vec = pl.BlockSpec((1, HEAD_DIM), lambda b, h: (0, 0))
    cache_spec = pl.BlockSpec((None, None, n_past, HEAD_DIM), lambda b, h: (b, layer, 0, h))
    table = pl.BlockSpec((s_len, HEAD_DIM), lambda b, h: (0, 0))
    in_specs = [
        pl.BlockSpec((s_len, qw), lambda b, h: (b, col0 // qw + h)),
        pl.BlockSpec((s_len, HEAD_DIM), lambda b, h: (b, k0 + h)),
        pl.BlockSpec((s_len, HEAD_DIM), lambda b, h: (b, v0 + h)),
        cache_spec, cache_spec, table, table, vec, vec,
    ]
    args = [p, p, p, cache_k, cache_v, cos, sin_signed,
            qn.reshape(1, HEAD_DIM), kn.reshape(1, HEAD_DIM)]
    if has_sink:
        in_specs.append(pl.BlockSpec(memory_space=pltpu.SMEM))
        args.append(sink)
    return pl.pallas_call(
        functools.partial(_attn_lat_kernel, groups=groups, has_sink=has_sink, window=window,
                          n_past=n_past, q_block=ATTN_Q_BLOCK),
        grid=(m // s_len, n_kv),
        in_specs=in_specs,
        out_specs=pl.BlockSpec((s_len, qw), lambda b, h: (b, h)),
        out_shape=jax.ShapeDtypeStruct((m, n_kv * qw), BF16),
        compiler_params=_params("parallel", "parallel"),
        name="attn_lat",
    )(*args)


def _rope_tables(s_len):
    quarter = HEAD_DIM // 4
    inv = ROPE_THETA ** (-jnp.arange(quarter, dtype=F32) / quarter)
    t = jnp.arange(s_len)
    ang_row = (t // GRID_W).astype(F32)[:, None] * inv[None, :]
    ang_col = (t % GRID_W).astype(F32)[:, None] * inv[None, :]
    cos = jnp.concatenate([jnp.cos(ang_row)] * 2 + [jnp.cos(ang_col)] * 2, axis=-1)
    sin = jnp.concatenate([-jnp.sin(ang_row), jnp.sin(ang_row),
                           -jnp.sin(ang_col), jnp.sin(ang_col)], axis=-1)
    return cos, sin


def _fourier_kernel(x_ref, cs_ref, f_ref, o_ref, *, ch, scale):
    s_len = f_ref.shape[0]
    cs = cs_ref[...]
    for s0 in range(0, x_ref.shape[0], s_len):
        x = x_ref[s0:s0 + s_len, :].astype(BF16)
        parts = [_dot(x[:, g * ch:(g + 1) * ch], cs) for g in range(x.shape[1] // ch)]
        y = jnp.concatenate(
            [jnp.concatenate([t[:, :ch] for t in parts], axis=1),
             jnp.concatenate([t[:, ch:] for t in parts], axis=1)], axis=0).astype(BF16)
        o_ref[s0:s0 + s_len, :] = (_dot(f_ref[...], y) * scale).astype(o_ref.dtype)


def _dft_cos_sin(n):
    j = np.arange(n, dtype=np.int64)
    ang = ((j[:, None] * j[None, :]) % n).astype(np.float64) * (2.0 * math.pi / n)
    return np.cos(ang).astype(np.float32), np.sin(ang).astype(np.float32)


def _dft_mats(ch, s_len):
    cc, sc = _dft_cos_sin(ch)
    cp, sp = _dft_cos_sin(s_len)
    return (jnp.asarray(np.concatenate([cc, sc], axis=1), BF16),
            jnp.asarray(np.concatenate([cp, -sp], axis=1), BF16))


def _fourier(p, col0, width, ch, s_len, cs_mat, f_mat):
    m = p.shape[0]
    tw = MIXER_COLS
    rows = max(s_len, MIXER_ROWS)
    return pl.pallas_call(
        functools.partial(_fourier_kernel, ch=ch, scale=1.0 / math.sqrt(s_len * ch)),
        grid=(m // rows, width // tw),
        in_specs=[
            pl.BlockSpec((rows, tw), lambda b, j: (b, col0 // tw + j)),
            pl.BlockSpec((ch, 2 * ch), lambda b, j: (0, 0)),
            pl.BlockSpec((s_len, 2 * s_len), lambda b, j: (0, 0)),
        ],
        out_specs=pl.BlockSpec((rows, tw), lambda b, j: (b, j)),
        out_shape=jax.ShapeDtypeStruct((m, width), BF16),
        compiler_params=_params("parallel", "parallel"),
        name="fourier",
    )(p, cs_mat, f_mat)


def _gating_kernel(u_ref, v_ref, gn_ref, w_ref, b_ref, o_ref, *, ch):
    rows = u_ref.shape[0]
    n_chunks = rows // CHUNK
    u = _gelu_tanh(u_ref[...])
    v = _gelu_tanh(v_ref[...])
    for g in range(u.shape[1] // ch):
        lanes = slice(g * ch, (g + 1) * ch)
        vn = _rms(v[:, lanes], gn_ref[:, lanes]).astype(BF16)
        rhs = jnp.concatenate([vn[c * CHUNK:(c + 1) * CHUNK] for c in range(n_chunks)], axis=1)
        sv = _dot(w_ref[g].astype(BF16), rhs)
        bias = b_ref[g]
        for c in range(n_chunks):
            rs = slice(c * CHUNK, (c + 1) * CHUNK)
            o_ref[rs, lanes] = (u[rs, lanes] * (sv[:, c * ch:(c + 1) * ch] + bias)).astype(o_ref.dtype)


def _gating(p, u_col0, v_col0, width, ch, gn, w_s, b_s, layer):
    m = p.shape[0]
    tw = MIXER_COLS
    rows = MIXER_ROWS
    gpt = tw // ch
    return pl.pallas_call(
        functools.partial(_gating_kernel, ch=ch),
        grid=(m // rows, width // tw),
        in_specs=[
            pl.BlockSpec((rows, tw), lambda i, j: (i, u_col0 // tw + j)),
            pl.BlockSpec((rows, tw), lambda i, j: (i, v_col0 // tw + j)),
            pl.BlockSpec((None, 1, tw), lambda i, j: (layer, 0, j)),
            pl.BlockSpec((None, gpt, CHUNK, CHUNK), lambda i, j: (layer, j, 0, 0)),
            pl.BlockSpec((None, gpt, CHUNK, ch), lambda i, j: (layer, j, 0, 0)),
        ],
        out_specs=pl.BlockSpec((rows, tw), lambda i, j: (i, j)),
        out_shape=jax.ShapeDtypeStruct((m, width), BF16),
        compiler_params=_params("parallel", "parallel"),
        name="spatial_gating",
    )(p, p, gn, w_s, b_s)


def kernel(x_prompt, x_sample, c, cache_ga_k, cache_ga_v, cache_sw_k, cache_sw_v, c_ctx, w_ada, b_ada, g_mix, w_in, qn_a, kn_a, qn_d, kn_d, sink_d, gn_c, w_s, b_s, w_o, g_ffn, w_up, conv_w, conv_b, w_down):
    n_layers = w_ada.shape[0]
    n_ctx, s_ctx, d = x_prompt.shape
    n_lat, s_lat, _ = x_sample.shape
    gw = d // N_MIXERS
    n_heads = gw // HEAD_DIM
    a_kv = cache_ga_k.shape[3]
    d_kv = cache_sw_k.shape[3]
    b_ch = gw // FOURIER_GROUPS
    c_ch = gn_c.shape[2]
    n_past = cache_ga_k.shape[2]
    assert n_lat + 1 <= MOD_ROWS and s_lat % GRID_W == 0

    a_col = 0
    b_col = a_col + gw + 2 * a_kv * HEAD_DIM
    u_col = b_col + gw
    v_col = u_col + gw
    d_col = v_col + gw

    w_in_b = [w_in[0].astype(BF16)] + [None] * (n_layers - 1)
    w_o_b = [w_o[0].astype(BF16)] + [None] * (n_layers - 1)
    w_up_b = [None] * n_layers
    w_down_b = [None] * n_layers

    cond = jnp.concatenate([c_ctx[None, :], c], axis=0)
    cond = jnp.pad(cond, ((0, MOD_ROWS - cond.shape[0]), (0, 0)))
    mod = _ada_mod(cond, w_ada, b_ada).reshape(n_layers, MOD_ROWS, N_MOD, 1, d)

    cos, sin_signed = _rope_tables(s_lat)
    dft = {s: _dft_mats(b_ch, s) for s in {s_ctx, s_lat}}
    gn = gn_c.reshape(n_layers, 1, gw)
    b_sb = jnp.broadcast_to(jnp.swapaxes(b_s, 1, 2)[..., None], b_s.shape[:1] + (b_s.shape[2], CHUNK, c_ch))
    cache = [t.reshape(t.shape[0], t.shape[1], n_past, -1)
             for t in (cache_ga_k, cache_ga_v, cache_sw_k, cache_sw_v)]

    ctx_row = lambda i, bm: 0
    lat_row = lambda i, bm: 1 + (i * bm) // s_lat

    def layer(x, l, latent, new_kv=None):
        row_of = lat_row if latent else ctx_row
        s_len = s_lat if latent else s_ctx
        h = _norm_mod(x, g_mix, mod, l, 1, 0, row_of)
        side = [(w_up, l)] if w_up_b[l] is None else []
        p, rounded = _in_proj(h, w_in_b[l], side)
        if side:
            w_up_b[l] = rounded[0]
        if latent:
            oa = _attn_lat(p, a_col, a_kv, n_heads // a_kv, s_len, cache[0], cache[1], l, cos,
                           sin_signed, qn_a[l], kn_a[l], None, None)
            od = _attn_lat(p, d_col, d_kv, n_heads // d_kv, s_len, cache[2], cache[3], l, cos,
                           sin_signed, qn_d[l], kn_d[l], sink_d[l], WINDOW)
        else:
            prev_a, prev_d = (None, None) if new_kv is None else (new_kv[:2], new_kv[2:])
            oa, ka, va = _attn_ctx(p, a_col, a_kv, n_heads // a_kv, s_len, qn_a[l], kn_a[l], None,
                                   l, n_layers, prev_a)
            od, kd, vd = _attn_ctx(p, d_col, d_kv, n_heads // d_kv, s_len, qn_d[l], kn_d[l],
                                   sink_d[l], l, n_layers, prev_d)
            new_kv = (ka, va, kd, vd)
        ob = _fourier(p, b_col, gw, b_ch, s_len, *dft[s_len])
        oc = _gating(p, u_col, v_col, gw, c_ch, gn, w_s, b_sb, l)
        x = _out_proj((oa, ob, oc, od), w_o_b[l], x, mod, l, 2, row_of)
        h2 = _norm_mod(x, g_ffn, mod, l, 4, 3, row_of)
        if w_down_b[l] is None:
            more = l + 1 < n_layers
            side = [(w_down, l)] + ([(w_up, l + 1)] if more else [])
            act, rounded = _ffn_up(h2, w_up_b[l], conv_w, conv_b, l, s_len, side)
            w_down_b[l] = rounded[0]
            side = [(w_in, l + 1), (w_o, l + 1)] if more else []
            x, rounded_next = _ffn_down(act, w_down_b[l], x, mod, l, 5, row_of, side)
            if more:
                w_up_b[l + 1] = rounded[1]
                w_in_b[l + 1], w_o_b[l + 1] = rounded_next
        else:
            act, _ = _ffn_up(h2, w_up_b[l], conv_w, conv_b, l, s_len)
            x, _ = _ffn_down(act, w_down_b[l], x, mod, l, 5, row_of)
        return x, new_kv

    xc = x_prompt.reshape(n_ctx * s_ctx, d)
    xl = x_sample.reshape(n_lat * s_lat, d)
    new_kv = None
    for l in range(n_layers):
        xc, new_kv = layer(xc, l, False, new_kv)
    for l in range(n_layers):
        xl, _ = layer(xl, l, True)

    return (xc.reshape(n_ctx, s_ctx, d), xl.reshape(n_lat, s_lat, d)) + tuple(new_kv)
```

```python
import functools
import math

import jax
import jax.numpy as jnp
import numpy as np
from jax import lax
from jax.experimental import pallas as pl
from jax.experimental.pallas import tpu as pltpu

F32 = jnp.float32
BF16 = jnp.bfloat16

HEAD_DIM = 128
GRID_W = 64
CHUNK = 128
WINDOW = 128
ROPE_THETA = 10000.0
EPS = 1e-6
N_MIXERS = 4
FOURIER_GROUPS = 8
N_MOD = 6
MOD_ROWS = 16

V7X_VMEM_BYTES = 64 * 1024 * 1024
VMEM_LIMIT = V7X_VMEM_BYTES - 8 * 1024 * 1024

TILE_M = 1024
TILE_N = 1024
TILE_N_OUT = 512
TILE_FF = 512
FFN_ROW_PIECES = 8
TILE_K_DOWN = 2560
LANES = 128
BF16_ROWS = 16
MIXER_ROWS = 1024
MIXER_COLS = 512
NORM_ROWS = 1024
NORM_COLS = 512
NORM_PIECE = 128
ATTN_Q_BLOCK = 256
LOG2E = math.log2(math.e)
MASKED = -1e30


def _params(*sem):
    return pltpu.CompilerParams(dimension_semantics=sem, vmem_limit_bytes=VMEM_LIMIT)


def _dot(a, b):
    return jnp.dot(a, b, preferred_element_type=F32)


def _rms(x, gain):
    ms = jnp.mean(x * x, axis=-1, keepdims=True)
    return (x * lax.rsqrt(ms + EPS)) * gain


def _silu(x):
    return x * (1.0 / (1.0 + jnp.exp(-x)))


def _gelu_tanh(x):
    a = -2.0 * math.sqrt(2.0 / math.pi) * LOG2E
    return x / (1.0 + jnp.exp2(x * (a + (a * 0.044715) * (x * x))))


def _ada_kernel(c_ref, w_ref, b_ref, o_ref):
    a = _silu(c_ref[...]).astype(BF16)
    o_ref[...] = _dot(a, w_ref[...].astype(BF16)) + b_ref[...]


def _ada_mod(cond, w_ada, b_ada):
    n_layers, d, n = w_ada.shape
    tn = TILE_N
    return pl.pallas_call(
        _ada_kernel,
        grid=(n_layers, n // tn),
        in_specs=[
            pl.BlockSpec((MOD_ROWS, d), lambda l, j: (0, 0)),
            pl.BlockSpec((None, d, tn), lambda l, j: (l, 0, j)),
            pl.BlockSpec((None, 1, tn), lambda l, j: (l, 0, j)),
        ],
        out_specs=pl.BlockSpec((None, MOD_ROWS, tn), lambda l, j: (l, 0, j)),
        out_shape=jax.ShapeDtypeStruct((n_layers, MOD_ROWS, n), F32),
        compiler_params=_params("parallel", "parallel"),
        name="ada_mod",
    )(cond, w_ada, b_ada.reshape(n_layers, 1, n))


def _mod_spec(layer, chunk, row_of, width, col_of):
    return pl.BlockSpec(
        (None, None, None, 1, width),
        lambda *g: (layer, row_of(*g), chunk, 0, col_of(*g)))


def _norm_mod_kernel(x_ref, g_ref, sc_ref, sh_ref, o_ref):
    rows, d = x_ref.shape
    for r0 in range(0, rows, NORM_PIECE):
        rs = slice(r0, r0 + NORM_PIECE)
        acc = jnp.zeros((NORM_PIECE, LANES), F32)
        for c0 in range(0, d, LANES):
            xc = x_ref[rs, c0:c0 + LANES]
            acc = acc + xc * xc
        inv = lax.rsqrt(jnp.sum(acc, axis=-1, keepdims=True) * (1.0 / d) + EPS)
        for c0 in range(0, d, NORM_COLS):
            cols = slice(c0, c0 + NORM_COLS)
            y = (x_ref[rs, cols] * inv) * g_ref[:, cols]
            o_ref[rs, cols] = (y * (1.0 + sc_ref[:, cols]) + sh_ref[:, cols]).astype(o_ref.dtype)


def _norm_mod(x, gain, mod, layer, sc_chunk, sh_chunk, row_of_tile):
    m, d = x.shape
    bm = NORM_ROWS
    row_of = lambda i: row_of_tile(i, bm)
    zero = lambda i: 0
    return pl.pallas_call(
        _norm_mod_kernel,
        grid=(m // bm,),
        in_specs=[
            pl.BlockSpec((bm, d), lambda i: (i, 0)),
            pl.BlockSpec((None, 1, d), lambda i: (layer, 0, 0)),
            _mod_spec(layer, sc_chunk, row_of, d, zero),
            _mod_spec(layer, sh_chunk, row_of, d, zero),
        ],
        out_specs=pl.BlockSpec((bm, d), lambda i: (i, 0)),
        out_shape=jax.ShapeDtypeStruct((m, d), BF16),
        compiler_params=_params("parallel"),
        name="norm_mod",
    )(x, gain.reshape(gain.shape[0], 1, d), mod, mod)


def _row_pieces(rows, pieces):
    step = rows // pieces
    return [slice(r, r + step) for r in range(0, rows, step)]


def _proj_kernel(x_ref, w_ref, *rest):
    n_side = len(rest) // 2
    o_ref = rest[n_side]
    _round_side(rest[:n_side] + rest[n_side + 1:])
    o_ref[...] = _dot(x_ref[...], w_ref[...]).astype(o_ref.dtype)


def _weight_spec(w, block, index):
    if isinstance(w, tuple):
        layer = w[1]
        return pl.BlockSpec((None,) + block, lambda *g: (layer,) + index(*g))
    return pl.BlockSpec(block, index)


def _weight_array(w):
    return w[0] if isinstance(w, tuple) else w


def _in_proj(h, w, side=()):
    m, k = h.shape
    n = _weight_array(w).shape[-1]
    tn = TILE_N // 2 if side else TILE_N
    nj = n // tn
    side_in, side_out, side_shape = _side_specs(side, (m // TILE_M) * nj, lambda i, j: i * nj + j)
    out = pl.pallas_call(
        _proj_kernel,
        grid=(m // TILE_M, nj),
        in_specs=[
            pl.BlockSpec((TILE_M, k), lambda i, j: (i, 0)),
            _weight_spec(w, (k, tn), lambda i, j: (0, j)),
        ] + side_in,
        out_specs=[pl.BlockSpec((TILE_M, tn), lambda i, j: (i, j))] + side_out,
        out_shape=[jax.ShapeDtypeStruct((m, n), F32)] + side_shape,
        compiler_params=_params(*(("arbitrary",) * 2 if side else ("parallel",) * 2)),
        name="in_proj",
    )(h, _weight_array(w), *[w_src for w_src, _ in side])
    return out[0], out[1:]


def _out_proj_kernel(a_ref, b_ref, c_ref, d_ref, w_ref, x_ref, gt_ref, o_ref):
    mixed = jnp.concatenate([a_ref[...], b_ref[...], c_ref[...], d_ref[...]], axis=1)
    o_ref[...] = x_ref[...] + gt_ref[...] * _dot(mixed, w_ref[...])


def _out_proj(mixed, w, x, mod, layer, gt_chunk, row_of_tile):
    m, d = x.shape
    kw = mixed[0].shape[1]
    tn = TILE_N_OUT
    row_of = lambda i, j: row_of_tile(i, TILE_M)
    mix_spec = pl.BlockSpec((TILE_M, kw), lambda i, j: (i, 0))
    return pl.pallas_call(
        _out_proj_kernel,
        grid=(m // TILE_M, d // tn),
        in_specs=[mix_spec] * N_MIXERS + [
            _weight_spec(w, (N_MIXERS * kw, tn), lambda i, j: (0, j)),
            pl.BlockSpec((TILE_M, tn), lambda i, j: (i, j)),
            _mod_spec(layer, gt_chunk, row_of, tn, lambda i, j: j),
        ],
        out_specs=pl.BlockSpec((TILE_M, tn), lambda i, j: (i, j)),
        out_shape=jax.ShapeDtypeStruct((m, d), F32),
        compiler_params=_params("parallel", "parallel"),
        name="out_proj",
    )(*mixed, _weight_array(w), x, mod)


def _ffn_up_kernel(h_ref, wg_ref, wu_ref, cwg_ref, cwu_ref, cbg_ref, cbu_ref, *rest, seq_len):
    n_side = len(rest) // 2
    o_ref = rest[n_side]
    _round_side(rest[:n_side] + rest[n_side + 1:])
    h = h_ref[...]
    rows = h.shape[0]
    pos = lax.rem(lax.broadcasted_iota(jnp.int32, (rows, 1), 0), seq_len)
    first = pos == 0
    last = pos == seq_len - 1

    def conv(z, cw_ref, cb_ref):
        prev = jnp.where(first, 0.0, pltpu.roll(z, 1, 0))
        nxt = jnp.where(last, 0.0, pltpu.roll(z, rows - 1, 0))
        return prev * cw_ref[0:1, :] + z * cw_ref[1:2, :] + nxt * cw_ref[2:3, :] + cb_ref[...]

    def rows_dot(w_ref):
        w = w_ref[...]
        return jnp.concatenate([_dot(h[r], w) for r in _row_pieces(rows, FFN_ROW_PIECES)], axis=0)

    gate = conv(rows_dot(wg_ref), cwg_ref, cbg_ref)
    up = conv(rows_dot(wu_ref), cwu_ref, cbu_ref)
    o_ref[...] = (_silu(gate) * up).astype(o_ref.dtype)


def _side_blocks(r, c, steps):
    options = []
    for cb in range(1, steps + 1):
        if c % cb or (c // cb) % LANES:
            continue
        for rb in range(steps // cb, 0, -1):
            if r % rb == 0 and (r // rb) % BF16_ROWS == 0:
                options.append((rb * cb, -cb, rb))
                break
    _, neg_cb, rb = max(options)
    return rb, -neg_cb


def _side_specs(side, steps, step_of):
    side_in, side_out, side_shape = [], [], []
    for w_src, src_layer in side:
        _, r, c = w_src.shape
        rb, cb = _side_blocks(r, c, steps)

        def at(*g, rb=rb, cb=cb):
            t = jnp.minimum(step_of(*g), rb * cb - 1)
            return (t // cb, t % cb)

        block = (r // rb, c // cb)
        side_in.append(pl.BlockSpec((None,) + block, lambda *g, at=at, sl=src_layer: (sl,) + at(*g)))
        side_out.append(pl.BlockSpec(block, at))
        side_shape.append(jax.ShapeDtypeStruct((r, c), BF16))
    return side_in, side_out, side_shape


def _round_side(side_refs):
    n = len(side_refs) // 2
    for src_ref, dst_ref in zip(side_refs[:n], side_refs[n:]):
        dst_ref[...] = src_ref[...].astype(dst_ref.dtype)


def _ffn_up(h, w_up, conv_w, conv_b, layer, seq_len, side=()):
    m, k = h.shape
    w_up_arr = _weight_array(w_up)
    d_ff = w_up_arr.shape[-1] // 2
    tn = TILE_FF
    nj = d_ff // tn
    ni = m // TILE_M
    assert TILE_M % seq_len == 0
    w_spec = lambda off: _weight_spec(w_up, (k, tn), lambda i, j: (0, off + j))
    cw_spec = lambda off: pl.BlockSpec((None, conv_w.shape[1], tn), lambda i, j: (layer, 0, off + j))
    cb_spec = lambda off: pl.BlockSpec((None, 1, tn), lambda i, j: (layer, 0, off + j))
    conv_b3 = conv_b.reshape(conv_b.shape[0], 1, 2 * d_ff)
    side_in, side_out, side_shape = _side_specs(side, ni * nj, lambda i, j: i * nj + j)
    out = pl.pallas_call(
        functools.partial(_ffn_up_kernel, seq_len=seq_len),
        grid=(ni, nj),
        in_specs=[
            pl.BlockSpec((TILE_M, k), lambda i, j: (i, 0)),
            w_spec(0), w_spec(nj), cw_spec(0), cw_spec(nj), cb_spec(0), cb_spec(nj),
        ] + side_in,
        out_specs=[pl.BlockSpec((TILE_M, tn), lambda i, j: (i, j))] + side_out,
        out_shape=[jax.ShapeDtypeStruct((m, d_ff), BF16)] + side_shape,
        compiler_params=_params(*(("arbitrary",) * 2 if side else ("parallel",) * 2)),
        name="ffn_up",
    )(h, w_up_arr, w_up_arr, conv_w, conv_w, conv_b3, conv_b3, *[w_src for w_src, _ in side])
    return out[0], out[1:]


def _ffn_down_kernel(a_ref, w_ref, x_ref, gt_ref, *rest):
    n_side = (len(rest) - 2) // 2
    o_ref, acc_ref = rest[n_side], rest[-1]
    _round_side(rest[:n_side] + rest[n_side + 1:-1])
    k = pl.program_id(2)
    last = pl.num_programs(2) - 1

    @pl.when(k == 0)
    def _():
        acc_ref[...] = _dot(a_ref[...], w_ref[...])

    @pl.when((k > 0) & (k < last))
    def _():
        acc_ref[...] += _dot(a_ref[...], w_ref[...])

    @pl.when(k == last)
    def _():
        o_ref[...] = x_ref[...] + gt_ref[...] * (acc_ref[...] + _dot(a_ref[...], w_ref[...]))


def _ffn_down(act, w, x, mod, layer, gt_chunk, row_of_tile, side=()):
    m, d = x.shape
    d_ff = act.shape[1]
    tk = TILE_K_DOWN
    grid = (m // TILE_M, d // TILE_N, d_ff // tk)
    row_of = lambda i, j, k: row_of_tile(i, TILE_M)
    side_in, side_out, side_shape = _side_specs(
        side, grid[0] * grid[1] * grid[2], lambda i, j, k: (i * grid[1] + j) * grid[2] + k)
    out = pl.pallas_call(
        _ffn_down_kernel,
        grid=grid,
        in_specs=[
            pl.BlockSpec((TILE_M, tk), lambda i, j, k: (i, k)),
            _weight_spec(w, (tk, TILE_N), lambda i, j, k: (k, j)),
            pl.BlockSpec((TILE_M, TILE_N), lambda i, j, k: (i, j)),
            _mod_spec(layer, gt_chunk, row_of, TILE_N, lambda i, j, k: j),
        ] + side_in,
        out_specs=[pl.BlockSpec((TILE_M, TILE_N), lambda i, j, k: (i, j))] + side_out,
        out_shape=[jax.ShapeDtypeStruct((m, d), F32)] + side_shape,
        scratch_shapes=[pltpu.VMEM((TILE_M, TILE_N), F32)],
        compiler_params=_params(*(("arbitrary",) * 3 if side else ("parallel", "parallel", "arbitrary"))),
        name="ffn_down",
    )(act, _weight_array(w), x, mod, *[w_src for w_src, _ in side])
    return out[0], out[1:]


def _softmax_pv(q, kt, v_ones, bias, sink):
    d = q.shape[1]
    scale = d ** -0.5
    s = _dot(q, kt)
    if bias is not None:
        s = s + bias
    m = jnp.max(s, axis=-1, keepdims=True) * scale
    if sink is not None:
        m = jnp.maximum(m, sink)
    m2 = m * LOG2E
    e = jnp.exp2(s * (scale * LOG2E) - m2).astype(BF16)
    ov = _dot(e, v_ones)
    den = ov[:, d:]
    if sink is not None:
        den = den + jnp.exp2(sink * LOG2E - m2)
    return ov[:, :d] / den


def _with_ones(v):
    return jnp.concatenate([v, jnp.ones_like(v)], axis=1)


def _attn_ctx_kernel(*refs, n_kv, groups, has_sink):
    qkv_ref, qn_ref, kn_ref = refs[:3]
    sink_ref = refs[3] if has_sink else None
    o_ref, ko_ref, vo_ref = refs[-3:]
    qn = qn_ref[...]
    kn = kn_ref[...]
    k0 = n_kv * groups * HEAD_DIM
    v0 = k0 + n_kv * HEAD_DIM
    for h in range(n_kv):
        k = _rms(qkv_ref[:, k0 + h * HEAD_DIM:k0 + (h + 1) * HEAD_DIM], kn)
        v = qkv_ref[:, v0 + h * HEAD_DIM:v0 + (h + 1) * HEAD_DIM]
        ko_ref[:, h, :] = k
        vo_ref[:, h, :] = v
        kt = k.T.astype(BF16)
        v_ones = _with_ones(v.astype(BF16))
        for g in range(groups):
            head = h * groups + g
            cols = slice(head * HEAD_DIM, (head + 1) * HEAD_DIM)
            q = _rms(qkv_ref[:, cols], qn).astype(BF16)
            sink = sink_ref[head] if has_sink else None
            o_ref[:, cols] = _softmax_pv(q, kt, v_ones, None, sink).astype(o_ref.dtype)


def _attn_ctx(p, col0, n_kv, groups, s_len, qn, kn, sink):
    m = p.shape[0]
    qw = n_kv * groups * HEAD_DIM
    width = qw + 2 * n_kv * HEAD_DIM
    assert col0 % width == 0
    has_sink = sink is not None
    vec = pl.BlockSpec((1, HEAD_DIM), lambda b: (0, 0))
    in_specs = [pl.BlockSpec((s_len, width), lambda b: (b, col0 // width)), vec, vec]
    args = [p, qn.reshape(1, HEAD_DIM), kn.reshape(1, HEAD_DIM)]
    if has_sink:
        in_specs.append(pl.BlockSpec(memory_space=pltpu.SMEM))
        args.append(sink)
    kv_spec = pl.BlockSpec((None, s_len, n_kv, HEAD_DIM), lambda b: (b, 0, 0, 0))
    kv_shape = jax.ShapeDtypeStruct((m // s_len, s_len, n_kv, HEAD_DIM), F32)
    return pl.pallas_call(
        functools.partial(_attn_ctx_kernel, n_kv=n_kv, groups=groups, has_sink=has_sink),
        grid=(m // s_len,),
        in_specs=in_specs,
        out_specs=[pl.BlockSpec((s_len, qw), lambda b: (b, 0)), kv_spec, kv_spec],
        out_shape=[jax.ShapeDtypeStruct((m, qw), BF16), kv_shape, kv_shape],
        compiler_params=_params("parallel"),
        name="attn_ctx",
    )(*args)


def _rope(x, cos, sin_signed):
    lane = lax.broadcasted_iota(jnp.int32, x.shape, 1)
    partner = jnp.where((lane & (HEAD_DIM // 4)) == 0,
                        pltpu.roll(x, HEAD_DIM - HEAD_DIM // 4, 1),
                        pltpu.roll(x, HEAD_DIM // 4, 1))
    return x * cos + partner * sin_signed


def _attn_lat_kernel(*refs, groups, has_sink, window, n_past, q_block):
    if has_sink:
        (q_ref, k_ref, v_ref, kc_ref, vc_ref, cos_ref, sin_ref, qn_ref, kn_ref, sink_ref,
         o_ref) = refs
    else:
        q_ref, k_ref, v_ref, kc_ref, vc_ref, cos_ref, sin_ref, qn_ref, kn_ref, o_ref = refs
    s_len = q_ref.shape[0]
    kv = pl.program_id(1)
    qn = qn_ref[...]
    k = _rope(_rms(k_ref[...], kn_ref[...]), cos_ref[...], sin_ref[...])
    kt = jnp.concatenate([kc_ref[...].T, k.T], axis=1).astype(BF16)
    v_ones = _with_ones(jnp.concatenate([vc_ref[...], v_ref[...]], axis=0).astype(BF16))
    biases = {}
    for q0 in range(0, s_len, q_block):
        rows = slice(q0, q0 + q_block)
        if window is None:
            kt_blk, v_blk, bias = kt, v_ones, None
        else:
            span = q_block + 2 * window
            start = min(max(q0 - window, 0), s_len - span)
            band = slice(n_past + start, n_past + start + span)
            kt_blk = jnp.concatenate([kt[:, :n_past], kt[:, band]], axis=1)
            v_blk = jnp.concatenate([v_ones[:n_past], v_ones[band]], axis=0)
            if q0 - start not in biases:
                col = lax.broadcasted_iota(jnp.int32, (q_block, n_past + span), 1)
                row = lax.broadcasted_iota(jnp.int32, (q_block, n_past + span), 0)
                dist = jnp.abs((row + (q0 - start)) - (col - n_past))
                biases[q0 - start] = jnp.where((col < n_past) | (dist <= window), 0.0, MASKED)
            bias = biases[q0 - start]
        cos = cos_ref[rows, :]
        sin = sin_ref[rows, :]
        for g in range(groups):
            cols = slice(g * HEAD_DIM, (g + 1) * HEAD_DIM)
            q = _rope(_rms(q_ref[rows, cols], qn), cos, sin).astype(BF16)
            sink = sink_ref[kv * groups + g] if has_sink else None
            o_ref[rows, cols] = _softmax_pv(q, kt_blk, v_blk, bias, sink).astype(o_ref.dtype)


def _attn_lat(p, col0, n_kv, groups, s_len, cache_k, cache_v, layer, cos, sin_signed,
              qn, kn, sink, window):
    m = p.shape[0]
    qw = groups * HEAD_DIM
    n_past = cache_k.shape[2]
    k0 = (col0 + n_kv * qw) // HEAD_DIM
    v0 = k0 + n_kv
    has_sink = sink is not None
    vec = pl.BlockSpec((1, HEAD_DIM), lambda b, h: (0, 0))
    cache_spec = pl.BlockSpec((None, None, n_past, HEAD_DIM), lambda b, h: (b, layer, 0, h))
    table = pl.BlockSpec((s_len, HEAD_DIM), lambda b, h: (0, 0))
    in_specs = [
        pl.BlockSpec((s_len, qw), lambda b, h: (b, col0 // qw + h)),
        pl.BlockSpec((s_len, HEAD_DIM), lambda b, h: (b, k0 + h)),
        pl.BlockSpec((s_len, HEAD_DIM), lambda b, h: (b, v0 + h)),
        cache_spec, cache_spec, table, table, vec, vec,
    ]
    args = [p, p, p, cache_k, cache_v, cos, sin_signed,
            qn.reshape(1, HEAD_DIM), kn.reshape(1, HEAD_DIM)]
    if has_sink:
        in_specs.append(pl.BlockSpec(memory_space=pltpu.SMEM))
        args.append(sink)
    return pl.pallas_call(
        functools.partial(_attn_lat_kernel, groups=groups, has_sink=has_sink, window=window,
                          n_past=n_past, q_block=ATTN_Q_BLOCK),
        grid=(m // s_len, n_kv),
        in_specs=in_specs,
        out_specs=pl.BlockSpec((s_len, qw), lambda b, h: (b, h)),
        out_shape=jax.ShapeDtypeStruct((m, n_kv * qw), BF16),
        compiler_params=_params("parallel", "parallel"),
        name="attn_lat",
    )(*args)


def _rope_tables(s_len):
    quarter = HEAD_DIM // 4
    inv = ROPE_THETA ** (-jnp.arange(quarter, dtype=F32) / quarter)
    t = jnp.arange(s_len)
    ang_row = (t // GRID_W).astype(F32)[:, None] * inv[None, :]
    ang_col = (t % GRID_W).astype(F32)[:, None] * inv[None, :]
    cos = jnp.concatenate([jnp.cos(ang_row)] * 2 + [jnp.cos(ang_col)] * 2, axis=-1)
    sin = jnp.concatenate([-jnp.sin(ang_row), jnp.sin(ang_row),
                           -jnp.sin(ang_col), jnp.sin(ang_col)], axis=-1)
    return cos, sin


def _fourier_kernel(x_ref, cs_ref, f_ref, o_ref, *, ch, scale):
    s_len = f_ref.shape[0]
    cs = cs_ref[...]
    for s0 in range(0, x_ref.shape[0], s_len):
        x = x_ref[s0:s0 + s_len, :].astype(BF16)
        parts = [_dot(x[:, g * ch:(g + 1) * ch], cs) for g in range(x.shape[1] // ch)]
        y = jnp.concatenate(
            [jnp.concatenate([t[:, :ch] for t in parts], axis=1),
             jnp.concatenate([t[:, ch:] for t in parts], axis=1)], axis=0).astype(BF16)
        o_ref[s0:s0 + s_len, :] = (_dot(f_ref[...], y) * scale).astype(o_ref.dtype)


def _dft_cos_sin(n):
    j = np.arange(n, dtype=np.int64)
    ang = ((j[:, None] * j[None, :]) % n).astype(np.float64) * (2.0 * math.pi / n)
    return np.cos(ang).astype(np.float32), np.sin(ang).astype(np.float32)


def _dft_mats(ch, s_len):
    cc, sc = _dft_cos_sin(ch)
    cp, sp = _dft_cos_sin(s_len)
    return (jnp.asarray(np.concatenate([cc, sc], axis=1), BF16),
            jnp.asarray(np.concatenate([cp, -sp], axis=1), BF16))


def _fourier(p, col0, width, ch, s_len, cs_mat, f_mat):
    m = p.shape[0]
    tw = MIXER_COLS
    rows = max(s_len, MIXER_ROWS)
    return pl.pallas_call(
        functools.partial(_fourier_kernel, ch=ch, scale=1.0 / math.sqrt(s_len * ch)),
        grid=(m // rows, width // tw),
        in_specs=[
            pl.BlockSpec((rows, tw), lambda b, j: (b, col0 // tw + j)),
            pl.BlockSpec((ch, 2 * ch), lambda b, j: (0, 0)),
            pl.BlockSpec((s_len, 2 * s_len), lambda b, j: (0, 0)),
        ],
        out_specs=pl.BlockSpec((rows, tw), lambda b, j: (b, j)),
        out_shape=jax.ShapeDtypeStruct((m, width), BF16),
        compiler_params=_params("parallel", "parallel"),
        name="fourier",
    )(p, cs_mat, f_mat)


def _gating_kernel(u_ref, v_ref, gn_ref, w_ref, b_ref, o_ref, *, ch):
    rows = u_ref.shape[0]
    n_chunks = rows // CHUNK
    u = _gelu_tanh(u_ref[...])
    v = _gelu_tanh(v_ref[...])
    for g in range(u.shape[1] // ch):
        lanes = slice(g * ch, (g + 1) * ch)
        vn = _rms(v[:, lanes], gn_ref[:, lanes]).astype(BF16)
        rhs = jnp.concatenate([vn[c * CHUNK:(c + 1) * CHUNK] for c in range(n_chunks)], axis=1)
        sv = _dot(w_ref[g].astype(BF16), rhs)
        bias = b_ref[g]
        for c in range(n_chunks):
            rs = slice(c * CHUNK, (c + 1) * CHUNK)
            o_ref[rs, lanes] = (u[rs, lanes] * (sv[:, c * ch:(c + 1) * ch] + bias)).astype(o_ref.dtype)


def _gating(p, u_col0, v_col0, width, ch, gn, w_s, b_s, layer):
    m = p.shape[0]
    tw = MIXER_COLS
    rows = MIXER_ROWS
    gpt = tw // ch
    return pl.pallas_call(
        functools.partial(_gating_kernel, ch=ch),
        grid=(m // rows, width // tw),
        in_specs=[
            pl.BlockSpec((rows, tw), lambda i, j: (i, u_col0 // tw + j)),
            pl.BlockSpec((rows, tw), lambda i, j: (i, v_col0 // tw + j)),
            pl.BlockSpec((None, 1, tw), lambda i, j: (layer, 0, j)),
            pl.BlockSpec((None, gpt, CHUNK, CHUNK), lambda i, j: (layer, j, 0, 0)),
            pl.BlockSpec((None, gpt, CHUNK, ch), lambda i, j: (layer, j, 0, 0)),
        ],
        out_specs=pl.BlockSpec((rows, tw), lambda i, j: (i, j)),
        out_shape=jax.ShapeDtypeStruct((m, width), BF16),
        compiler_params=_params("parallel", "parallel"),
        name="spatial_gating",
    )(p, p, gn, w_s, b_s)


def kernel(x_prompt, x_sample, c, cache_ga_k, cache_ga_v, cache_sw_k, cache_sw_v, c_ctx, w_ada, b_ada, g_mix, w_in, qn_a, kn_a, qn_d, kn_d, sink_d, gn_c, w_s, b_s, w_o, g_ffn, w_up, conv_w, conv_b, w_down):
    n_layers = w_ada.shape[0]
    n_ctx, s_ctx, d = x_prompt.shape
    n_lat, s_lat, _ = x_sample.shape
    gw = d // N_MIXERS
    n_heads = gw // HEAD_DIM
    a_kv = cache_ga_k.shape[3]
    d_kv = cache_sw_k.shape[3]
    b_ch = gw // FOURIER_GROUPS
    c_ch = gn_c.shape[2]
    n_past = cache_ga_k.shape[2]
    assert n_lat + 1 <= MOD_ROWS and s_lat % GRID_W == 0

    a_col = 0
    b_col = a_col + gw + 2 * a_kv * HEAD_DIM
    u_col = b_col + gw
    v_col = u_col + gw
    d_col = v_col + gw

    w_in_b = [w_in[0].astype(BF16)] + [None] * (n_layers - 1)
    w_o_b = [w_o[0].astype(BF16)] + [None] * (n_layers - 1)
    w_up_b = [None] * n_layers
    w_down_b = [None] * n_layers

    cond = jnp.concatenate([c_ctx[None, :], c], axis=0)
    cond = jnp.pad(cond, ((0, MOD_ROWS - cond.shape[0]), (0, 0)))
    mod = _ada_mod(cond, w_ada, b_ada).reshape(n_layers, MOD_ROWS, N_MOD, 1, d)

    cos, sin_signed = _rope_tables(s_lat)
    dft = {s: _dft_mats(b_ch, s) for s in {s_ctx, s_lat}}
    gn = gn_c.reshape(n_layers, 1, gw)
    b_sb = jnp.broadcast_to(jnp.swapaxes(b_s, 1, 2)[..., None], b_s.shape[:1] + (b_s.shape[2], CHUNK, c_ch))
    cache = [t.reshape(t.shape[0], t.shape[1], n_past, -1)
             for t in (cache_ga_k, cache_ga_v, cache_sw_k, cache_sw_v)]

    ctx_row = lambda i, bm: 0
    lat_row = lambda i, bm: 1 + (i * bm) // s_lat

    def layer(x, l, latent):
        row_of = lat_row if latent else ctx_row
        s_len = s_lat if latent else s_ctx
        h = _norm_mod(x, g_mix, mod, l, 1, 0, row_of)
        side = [(w_up, l)] if w_up_b[l] is None else []
        p, rounded = _in_proj(h, w_in_b[l], side)
        if side:
            w_up_b[l] = rounded[0]
        if latent:
            oa = _attn_lat(p, a_col, a_kv, n_heads // a_kv, s_len, cache[0], cache[1], l, cos,
                           sin_signed, qn_a[l], kn_a[l], None, None)
            od = _attn_lat(p, d_col, d_kv, n_heads // d_kv, s_len, cache[2], cache[3], l, cos,
                           sin_signed, qn_d[l], kn_d[l], sink_d[l], WINDOW)
            new_kv = None
        else:
            oa, ka, va = _attn_ctx(p, a_col, a_kv, n_heads // a_kv, s_len, qn_a[l], kn_a[l], None)
            od, kd, vd = _attn_ctx(p, d_col, d_kv, n_heads // d_kv, s_len, qn_d[l], kn_d[l], sink_d[l])
            new_kv = (ka, va, kd, vd)
        ob = _fourier(p, b_col, gw, b_ch, s_len, *dft[s_len])
        oc = _gating(p, u_col, v_col, gw, c_ch, gn, w_s, b_sb, l)
        x = _out_proj((oa, ob, oc, od), w_o_b[l], x, mod, l, 2, row_of)
        h2 = _norm_mod(x, g_ffn, mod, l, 4, 3, row_of)
        if w_down_b[l] is None:
            more = l + 1 < n_layers
            side = [(w_down, l)] + ([(w_up, l + 1)] if more else [])
            act, rounded = _ffn_up(h2, w_up_b[l], conv_w, conv_b, l, s_len, side)
            w_down_b[l] = rounded[0]
            side = [(w_in, l + 1), (w_o, l + 1)] if more else []
            x, rounded_next = _ffn_down(act, w_down_b[l], x, mod, l, 5, row_of, side)
            if more:
                w_up_b[l + 1] = rounded[1]
                w_in_b[l + 1], w_o_b[l + 1] = rounded_next
        else:
            act, _ = _ffn_up(h2, w_up_b[l], conv_w, conv_b, l, s_len)
            x, _ = _ffn_down(act, w_down_b[l], x, mod, l, 5, row_of)
        return x, new_kv

    xc = x_prompt.reshape(n_ctx * s_ctx, d)
    xl = x_sample.reshape(n_lat * s_lat, d)
    new_kv = []
    for l in range(n_layers):
        xc, kv_l = layer(xc, l, False)
        new_kv.append(kv_l)
    for l in range(n_layers):
        xl, _ = layer(xl, l, True)

    stacked = tuple(jnp.stack([kv_l[i] for kv_l in new_kv], axis=1) for i in range(4))
    return (xc.reshape(n_ctx, s_ctx, d), xl.reshape(n_lat, s_lat, d)) + stacked
```
